```python
import math
import jax
import jax.numpy as jnp
from jax import lax
import numpy as np

D_MODEL = 1024
BATCH = 8
SEQ = 4096
DEPTH = 2

GRID_W = 64
CTX_LEN = 256
EPS = 1e-6
N_BRANCH = 3
F32 = jnp.float32

CONV_DIM = 512
CONV_WIDTH = 31

GLA_HEADS = 4
GLA_DK = 64
GLA_DV = 128
GLA_RANK = 16
GLA_GATE_NORM = 16.0
GLA_CHUNK = 64
GLA_QK = GLA_HEADS * GLA_DK
GLA_V = GLA_HEADS * GLA_DV
ROPE_BASE = 10000.0

NA_HEADS = 8
NA_HD = 64
NA_DIM = NA_HEADS * NA_HD
NA_KH = 8
NA_KW = 16
NA_QR = 8
NA_QC = 16

PEER_HEADS = 8
PEER_NKEYS = 128
PEER_EXPERTS = PEER_NKEYS * PEER_NKEYS
PEER_TOPK = 16
PEER_DKEY = 256
PEER_CHUNK = 128

IN_SPLITS = (2 * CONV_DIM, GLA_QK, GLA_QK, GLA_V, GLA_V, 2 * GLA_RANK, NA_DIM, NA_DIM, NA_DIM, N_BRANCH * D_MODEL)
IN_DIM = sum(IN_SPLITS)

kernel_name = 'hybrid_conv_gla_natten_peer_dit'


def rmsnorm(x, g):
    xf = x.astype(F32)
    y = xf * lax.rsqrt(jnp.mean(xf * xf, axis=-1, keepdims=True) + EPS)
    return (y * g.astype(F32)).astype(x.dtype)


def layernorm(x, g, b):
    xf = x.astype(F32)
    mu = jnp.mean(xf, axis=-1, keepdims=True)
    xc = xf - mu
    var = jnp.mean(xc * xc, axis=-1, keepdims=True)
    return (xc * lax.rsqrt(var + EPS) * g.astype(F32) + b.astype(F32)).astype(x.dtype)


def split_cols(p):
    return jnp.split(p, np.cumsum(IN_SPLITS)[:-1].tolist(), axis=-1)


def conv_module(glu_in, conv_w, conv_b, ln_g, ln_b, w_o):
    u = glu_in[..., :CONV_DIM] * jax.nn.sigmoid(glu_in[..., CONV_DIM:])
    u = lax.conv_general_dilated(
        u, conv_w.astype(u.dtype)[:, None, :], (1,), [(CONV_WIDTH // 2, CONV_WIDTH // 2)],
        dimension_numbers=('NWC', 'WIO', 'NWC'), feature_group_count=CONV_DIM) + conv_b.astype(u.dtype)
    u = jax.nn.silu(layernorm(u, ln_g, ln_b))
    return u @ w_o


def axial_rope(L):
    t = jnp.arange(L)
    row = (t // GRID_W).astype(F32)
    col = (t % GRID_W).astype(F32)
    n_freq = GLA_DK // 4
    inv = ROPE_BASE ** (-jnp.arange(n_freq, dtype=F32) / n_freq)
    ang = jnp.concatenate([row[:, None] * inv, col[:, None] * inv], axis=-1)
    return jnp.cos(ang), jnp.sin(ang)


def apply_rope(x, cos, sin):
    x1, x2 = x[..., 0::2], x[..., 1::2]
    c = cos[None, :, None, :]
    s = sin[None, :, None, :]
    return jnp.stack([x1 * c - x2 * s, x1 * s + x2 * c], axis=-1).reshape(x.shape)


def gla_features(pq, pk, pv, pa, wa2, ba, rope):
    B, L, _ = pq.shape
    q = pq.astype(F32).reshape(B, L, GLA_HEADS, GLA_DK) * GLA_DK ** -0.5
    k = pk.astype(F32).reshape(B, L, GLA_HEADS, GLA_DK)
    if rope is not None:
        q = apply_rope(q, *rope)
        k = apply_rope(k, *rope)
    v = pv.astype(F32).reshape(B, L, GLA_HEADS, GLA_DV)
    z = jnp.einsum('blzr,zrk->blzk', pa.astype(F32).reshape(B, L, 2, GLA_RANK), wa2.astype(F32)) + ba.astype(F32)
    log_a = (jax.nn.log_sigmoid(z) / GLA_GATE_NORM).reshape(B, L, 2, GLA_HEADS, GLA_DK)
    return q, k, v, log_a[:, :, 0], log_a[:, :, 1]


def gla_scan(q, k, v, log_a, s0):
    B, L, H, _ = q.shape
    DV = v.shape[-1]
    C = GLA_CHUNK
    n = L // C

    def blocks(t):
        return t.reshape(B, n, C, H, t.shape[-1]).transpose(1, 0, 3, 2, 4)

    lower = jnp.tril(jnp.ones((C, C), dtype=bool))

    def step(state, inp):
        qb, kb, vb, ab = inp
        b = jnp.cumsum(ab, axis=2)
        b_mid = b[:, :, C // 2:C // 2 + 1]
        b_end = b[:, :, C - 1:]
        o_inter = jnp.einsum('bhcd,bhde->bhce', qb * jnp.exp(b), state)
        att = jnp.einsum('bhid,bhjd->bhij', qb * jnp.exp(b - b_mid), kb * jnp.exp(b_mid - b))
        att = jnp.where(lower, att, 0.0)
        o_intra = jnp.einsum('bhij,bhje->bhie', att, vb)
        state = state * jnp.exp(b_end[:, :, 0])[..., None] + jnp.einsum('bhcd,bhce->bhde', kb * jnp.exp(b_end - b), vb)
        return state, o_inter + o_intra

    s_fin, o = lax.scan(step, s0, (blocks(q), blocks(k), blocks(v), blocks(log_a)))
    return o.transpose(1, 0, 3, 2, 4).reshape(B, L, H, DV), s_fin


def gla_bidir(q, k, v, la_f, la_b, s0_f, s0_b):
    o_f, s_f = gla_scan(q, k, v, la_f, s0_f)
    flip = lambda t: jnp.flip(t, axis=1)
    o_b, s_b = gla_scan(flip(q), flip(k), flip(v), flip(la_b), s0_b)
    return o_f + flip(o_b), s_f, s_b


def gla_output(o, pg, norm_g, w_o):
    B, L = o.shape[:2]
    on = rmsnorm(o, norm_g.reshape(GLA_HEADS, GLA_DV)).reshape(B, L, GLA_V)
    return (on.astype(pg.dtype) * jax.nn.silu(pg)) @ w_o


def na_latent(q, k, v, kc, vc, rpb):
    B, S, H, hd = q.shape
    rows = S // GRID_W
    kh = min(NA_KH, rows)
    kw, qc = NA_KW, NA_QC
    qr = math.gcd(rows, NA_QR)
    rr = min(qr + kh - 1, rows)
    cr = qc + kw - 1
    nrb, ncb = rows // qr, GRID_W // qc
    rs = np.clip(np.arange(rows) - kh // 2, 0, rows - kh)
    cs = np.clip(np.arange(GRID_W) - kw // 2, 0, GRID_W - kw)
    row_idx = np.minimum(rs[::qr], rows - rr)[:, None] + np.arange(rr)
    col_idx = np.minimum(cs[::qc], GRID_W - cr)[:, None] + np.arange(cr)
    q_rows = np.arange(rows).reshape(nrb, qr)
    q_cols = np.arange(GRID_W).reshape(ncb, qc)
    d_r = row_idx[:, None, :] - q_rows[:, :, None]
    d_c = col_idx[:, None, :] - q_cols[:, :, None]
    ok_r = (row_idx[:, None, :] >= rs[q_rows][:, :, None]) & (row_idx[:, None, :] < rs[q_rows][:, :, None] + kh)
    ok_c = (col_idx[:, None, :] >= cs[q_cols][:, :, None]) & (col_idx[:, None, :] < cs[q_cols][:, :, None] + kw)
    i_r = np.clip(d_r + NA_KH - 1, 0, 2 * NA_KH - 2)[:, :, :, None, None, None]
    i_c = np.clip(d_c + NA_KW - 1, 0, 2 * NA_KW - 2)[None, None, None]
    ok = ok_r[:, :, :, None, None, None] & ok_c[None, None, None]
    bias = jnp.where(ok, rpb.astype(F32)[:, i_r, i_c], -jnp.inf).transpose(0, 1, 4, 2, 5, 3, 6)
    n_lat = rr * cr

    def one_sample(args):
        qs, ks_, vs, kcs, vcs = args
        qg = qs.reshape(nrb, qr, ncb, qc, H, hd)
        kg = jnp.take(jnp.take(ks_.reshape(rows, GRID_W, H, hd), row_idx, axis=0), col_idx, axis=2)
        vg = jnp.take(jnp.take(vs.reshape(rows, GRID_W, H, hd), row_idx, axis=0), col_idx, axis=2)
        s_lat = jnp.einsum('xqyphd,xryshd->hxyqprs', qg, kg).astype(F32) + bias
        s_ctx = jnp.einsum('xqyphd,chd->hxyqpc', qg, kcs).astype(F32)
        prob = jax.nn.softmax(jnp.concatenate([s_lat.reshape(*s_lat.shape[:5], n_lat), s_ctx], axis=-1), axis=-1)
        p_lat = prob[..., :n_lat].reshape(s_lat.shape).astype(vs.dtype)
        p_ctx = prob[..., n_lat:].astype(vs.dtype)
        out = jnp.einsum('hxyqprs,xryshd->xqyphd', p_lat, vg) + jnp.einsum('hxyqpc,chd->xqyphd', p_ctx, vcs)
        return out.reshape(S, H * hd)

    return lax.map(one_sample, (q * NA_HD ** -0.5, k, v, kc, vc))


def na_context(qc, kc, vc):
    s = jnp.einsum('bihd,bjhd->bhij', qc * NA_HD ** -0.5, kc).astype(F32)
    p = jax.nn.softmax(s, axis=-1).astype(vc.dtype)
    o = jnp.einsum('bhij,bjhd->bihd', p, vc)
    return o.reshape(qc.shape[0], qc.shape[1], NA_DIM)


def merge_branches(pre_gate, b_gate, y_conv, y_gla, y_na, w_out):
    g = jax.nn.sigmoid((pre_gate + b_gate).astype(F32)).reshape(*pre_gate.shape[:-1], N_BRANCH, D_MODEL)
    m = g[..., 0, :] * y_conv + g[..., 1, :] * y_gla + g[..., 2, :] * y_na
    return m.astype(pre_gate.dtype) @ w_out


def token_mixer(h, hc, need_ctx, w_in, b_gate, conv_w, conv_b, conv_ln_g, conv_ln_b, w_conv_o,
                gla_wa2, gla_ba, gla_norm_g, w_gla_o, na_rpb, w_na_o, w_out):
    B, S, _ = h.shape
    Lc = hc.shape[1]
    l_glu, l_q, l_k, l_v, l_g, l_a, l_nq, l_nk, l_nv, l_br = split_cols(h @ w_in)
    c_glu, c_q, c_k, c_v, c_g, c_a, c_nq, c_nk, c_nv, c_br = split_cols(hc @ w_in)

    cq, ck, cv, cla_f, cla_b = gla_features(c_q, c_k, c_v, c_a, gla_wa2, gla_ba, None)
    zero = jnp.zeros((B, GLA_HEADS, GLA_DK, GLA_DV), F32)
    co_gla, s_f, s_b = gla_bidir(cq, ck, cv, cla_f, cla_b, zero, zero)
    q, k, v, la_f, la_b = gla_features(l_q, l_k, l_v, l_a, gla_wa2, gla_ba, axial_rope(S))
    o_gla, _, _ = gla_bidir(q, k, v, la_f, la_b, s_f, s_b)

    heads = lambda t, L: t.reshape(B, L, NA_HEADS, NA_HD)
    nkc, nvc = heads(c_nk, Lc), heads(c_nv, Lc)
    y_na = na_latent(heads(l_nq, S), heads(l_nk, S), heads(l_nv, S), nkc, nvc, na_rpb)

    y = merge_branches(l_br, b_gate,
                       conv_module(l_glu, conv_w, conv_b, conv_ln_g, conv_ln_b, w_conv_o),
                       gla_output(o_gla, l_g, gla_norm_g, w_gla_o),
                       y_na @ w_na_o, w_out)
    yc = None
    if need_ctx:
        yc = merge_branches(c_br, b_gate,
                            conv_module(c_glu, conv_w, conv_b, conv_ln_g, conv_ln_b, w_conv_o),
                            gla_output(co_gla, c_g, gla_norm_g, w_gla_o),
                            na_context(heads(c_nq, Lc), nkc, nvc) @ w_na_o, w_out)
    return y, yc


def peer_ffn(h, wq, sub_keys, u_tab, v_tab):
    B, L, D = h.shape
    T = B * L
    hf = h.reshape(T, D)
    qry = (hf @ wq).reshape(T, PEER_HEADS, 2, PEER_DKEY // 2)
    s = jnp.einsum('thzd,hzkd->thzk', qry, sub_keys).astype(F32)
    ts, ti = lax.top_k(s, PEER_TOPK)
    cand_s = (ts[:, :, 0, :, None] + ts[:, :, 1, None, :]).reshape(T, PEER_HEADS, PEER_TOPK * PEER_TOPK)
    cand_e = (ti[:, :, 0, :, None] * PEER_NKEYS + ti[:, :, 1, None, :]).reshape(T, PEER_HEADS, PEER_TOPK * PEER_TOPK)
    best_s, best_j = lax.top_k(cand_s, PEER_TOPK)
    experts = jnp.take_along_axis(cand_e, best_j, axis=-1)
    gates = jax.nn.softmax(best_s, axis=-1).astype(h.dtype)
    nb = T // PEER_CHUNK

    def block(args):
        hb, eb, gb = args
        act = jax.nn.gelu(jnp.einsum('td,thkd->thk', hb, jnp.take(u_tab, eb, axis=0)), approximate=False)
        return jnp.einsum('thk,thkd->td', gb * act, jnp.take(v_tab, eb, axis=0))

    out = lax.map(block, (hf.reshape(nb, PEER_CHUNK, D),
                          experts.reshape(nb, PEER_CHUNK, PEER_HEADS, PEER_TOPK),
                          gates.reshape(nb, PEER_CHUNK, PEER_HEADS, PEER_TOPK)))
    return out.reshape(B, L, D)


def setup_inputs(seed: int = 0) -> dict:
    key = jax.random.key(seed)
    ks = iter(jax.random.split(key, 40))
    D = D_MODEL
    L = DEPTH

    def nrm(shape, scale):
        return jax.random.normal(next(ks), shape, jnp.float32) * scale

    return {
        'x': nrm((BATCH, SEQ, D), 1.0),
        'c': nrm((BATCH, D), 1.0),
        'ctx': nrm((BATCH, CTX_LEN, D), 1.0),
        'c_ctx': nrm((D,), 1.0),
        'w_mod': nrm((L, D, 6 * D), 0.5 * D ** -0.5),
        'b_mod': nrm((L, 6 * D), 0.02),
        'norm1_g': 1.0 + nrm((L, D), 0.05),
        'norm2_g': 1.0 + nrm((L, D), 0.05),
        'w_in': nrm((L, D, IN_DIM), D ** -0.5),
        'b_gate': nrm((L, N_BRANCH * D), 0.02),
        'conv_w': nrm((L, CONV_WIDTH, CONV_DIM), CONV_WIDTH ** -0.5),
        'conv_b': nrm((L, CONV_DIM), 0.02),
        'conv_ln_g': 1.0 + nrm((L, CONV_DIM), 0.05),
        'conv_ln_b': nrm((L, CONV_DIM), 0.02),
        'w_conv_o': nrm((L, CONV_DIM, D), CONV_DIM ** -0.5),
        'gla_wa2': nrm((L, 2, GLA_RANK, GLA_QK), GLA_RANK ** -0.5),
        'gla_ba': nrm((L, 2, GLA_QK), 0.1),
        'gla_norm_g': 1.0 + nrm((L, GLA_V), 0.05),
        'w_gla_o': nrm((L, GLA_V, D), GLA_V ** -0.5),
        'na_rpb': nrm((L, NA_HEADS, 2 * NA_KH - 1, 2 * NA_KW - 1), 0.1),
        'w_na_o': nrm((L, NA_DIM, D), NA_DIM ** -0.5),
        'w_out': nrm((L, D, D), D ** -0.5),
        'peer_wq': nrm((L, D, PEER_HEADS * PEER_DKEY), D ** -0.5),
        'peer_keys': nrm((L, PEER_HEADS, 2, PEER_NKEYS, PEER_DKEY // 2), (PEER_DKEY // 2) ** -0.5),
        'peer_u': nrm((L, PEER_EXPERTS, D), D ** -0.5),
        'peer_v': nrm((L, PEER_EXPERTS, D), 1.0),
        'final_g': 1.0 + nrm((D,), 0.05),
    }


def reference(x, c, ctx, c_ctx, w_mod, b_mod, norm1_g, norm2_g, w_in, b_gate, conv_w, conv_b,
              conv_ln_g, conv_ln_b, w_conv_o, gla_wa2, gla_ba, gla_norm_g, w_gla_o, na_rpb, w_na_o,
              w_out, peer_wq, peer_keys, peer_u, peer_v, final_g):
    xc = ctx
    cond = jax.nn.silu(c.astype(F32))
    cond_ctx = jax.nn.silu(c_ctx.astype(F32))
    for i in range(DEPTH):
        last = i == DEPTH - 1
        mod = (cond @ w_mod[i].astype(F32) + b_mod[i].astype(F32)).astype(x.dtype)[:, None, :]
        modc = (cond_ctx @ w_mod[i].astype(F32) + b_mod[i].astype(F32)).astype(x.dtype)
        sh1, sc1, g1, sh2, sc2, g2 = jnp.split(mod, 6, axis=-1)
        csh1, csc1, cg1, csh2, csc2, cg2 = jnp.split(modc, 6, axis=-1)

        h = rmsnorm(x, norm1_g[i]) * (1 + sc1) + sh1
        hc = rmsnorm(xc, norm1_g[i]) * (1 + csc1) + csh1
        y, yc = token_mixer(h, hc, not last, w_in[i], b_gate[i], conv_w[i], conv_b[i], conv_ln_g[i],
                            conv_ln_b[i], w_conv_o[i], gla_wa2[i], gla_ba[i], gla_norm_g[i], w_gla_o[i],
                            na_rpb[i], w_na_o[i], w_out[i])
        x = x + g1 * y.astype(x.dtype)
        h2 = rmsnorm(x, norm2_g[i]) * (1 + sc2) + sh2
        x = x + g2 * peer_ffn(h2, peer_wq[i], peer_keys[i], peer_u[i], peer_v[i]).astype(x.dtype)
        if not last:
            xc = xc + cg1 * yc.astype(xc.dtype)
            hc2 = rmsnorm(xc, norm2_g[i]) * (1 + csc2) + csh2
            xc = xc + cg2 * peer_ffn(hc2, peer_wq[i], peer_keys[i], peer_u[i], peer_v[i]).astype(xc.dtype)
    return rmsnorm(x, final_g)
```

```python
import math
import functools
import jax
import jax.numpy as jnp
from jax import lax
import numpy as np
from jax.experimental import pallas as pl
from jax.experimental.pallas import tpu as pltpu

D_MODEL = 1024
BATCH = 8
SEQ = 4096
DEPTH = 2
GRID_W = 64
CTX_LEN = 256
EPS = 1e-6
N_BRANCH = 3
F32 = jnp.float32
CONV_DIM = 512
CONV_WIDTH = 31
GLA_HEADS = 4
GLA_DK = 64
GLA_DV = 128
GLA_RANK = 16
GLA_GATE_NORM = 16.0
GLA_CHUNK = 64
GLA_QK = GLA_HEADS * GLA_DK
GLA_V = GLA_HEADS * GLA_DV
ROPE_BASE = 10000.0
NA_HEADS = 8
NA_HD = 64
NA_DIM = NA_HEADS * NA_HD
NA_KH = 8
NA_KW = 16
NA_QR = 8
NA_QC = 16
PEER_HEADS = 8
PEER_NKEYS = 128
PEER_EXPERTS = PEER_NKEYS * PEER_NKEYS
PEER_TOPK = 16
PEER_DKEY = 256
PEER_CHUNK = 128
IN_SPLITS = (2 * CONV_DIM, GLA_QK, GLA_QK, GLA_V, GLA_V, 2 * GLA_RANK, NA_DIM, NA_DIM, NA_DIM, N_BRANCH * D_MODEL)
IN_DIM = sum(IN_SPLITS)


def rmsnorm(x, g):
    xf = x.astype(F32)
    y = xf * lax.rsqrt(jnp.mean(xf * xf, axis=-1, keepdims=True) + EPS)
    return (y * g.astype(F32)).astype(x.dtype)


def layernorm(x, g, b):
    xf = x.astype(F32)
    mu = jnp.mean(xf, axis=-1, keepdims=True)
    xc = xf - mu
    var = jnp.mean(xc * xc, axis=-1, keepdims=True)
    return (xc * lax.rsqrt(var + EPS) * g.astype(F32) + b.astype(F32)).astype(x.dtype)


def split_cols(p):
    return jnp.split(p, np.cumsum(IN_SPLITS)[:-1].tolist(), axis=-1)


def conv_module(glu_in, conv_w, conv_b, ln_g, ln_b, w_o):
    u = glu_in[..., :CONV_DIM] * jax.nn.sigmoid(glu_in[..., CONV_DIM:])
    u = lax.conv_general_dilated(
        u, conv_w.astype(u.dtype)[:, None, :], (1,), [(CONV_WIDTH // 2, CONV_WIDTH // 2)],
        dimension_numbers=('NWC', 'WIO', 'NWC'), feature_group_count=CONV_DIM) + conv_b.astype(u.dtype)
    u = jax.nn.silu(layernorm(u, ln_g, ln_b))
    return u @ w_o


def axial_rope(L):
    t = jnp.arange(L)
    row = (t // GRID_W).astype(F32)
    col = (t % GRID_W).astype(F32)
    n_freq = GLA_DK // 4
    inv = ROPE_BASE ** (-jnp.arange(n_freq, dtype=F32) / n_freq)
    ang = jnp.concatenate([row[:, None] * inv, col[:, None] * inv], axis=-1)
    return jnp.cos(ang), jnp.sin(ang)


def apply_rope(x, cos, sin):
    x1, x2 = x[..., 0::2], x[..., 1::2]
    c = cos[None, :, None, :]
    s = sin[None, :, None, :]
    return jnp.stack([x1 * c - x2 * s, x1 * s + x2 * c], axis=-1).reshape(x.shape)


def gla_features(pq, pk, pv, pa, wa2, ba, rope):
    B, L, _ = pq.shape
    q = pq.astype(F32).reshape(B, L, GLA_HEADS, GLA_DK) * GLA_DK ** -0.5
    k = pk.astype(F32).reshape(B, L, GLA_HEADS, GLA_DK)
    if rope is not None:
        q = apply_rope(q, *rope)
        k = apply_rope(k, *rope)
    v = pv.astype(F32).reshape(B, L, GLA_HEADS, GLA_DV)
    z = jnp.einsum('blzr,zrk->blzk', pa.astype(F32).reshape(B, L, 2, GLA_RANK), wa2.astype(F32)) + ba.astype(F32)
    log_a = (jax.nn.log_sigmoid(z) / GLA_GATE_NORM).reshape(B, L, 2, GLA_HEADS, GLA_DK)
    return q, k, v, log_a[:, :, 0], log_a[:, :, 1]


def gla_scan(q, k, v, log_a, s0):
    B, L, H, _ = q.shape
    DV = v.shape[-1]
    C = GLA_CHUNK
    n = L // C

    def blocks(t):
        return t.reshape(B, n, C, H, t.shape[-1]).transpose(1, 0, 3, 2, 4)

    lower = jnp.tril(jnp.ones((C, C), dtype=bool))

    def step(state, inp):
        qb, kb, vb, ab = inp
        b = jnp.cumsum(ab, axis=2)
        b_mid = b[:, :, C // 2:C // 2 + 1]
        b_end = b[:, :, C - 1:]
        o_inter = jnp.einsum('bhcd,bhde->bhce', qb * jnp.exp(b), state)
        att = jnp.einsum('bhid,bhjd->bhij', qb * jnp.exp(b - b_mid), kb * jnp.exp(b_mid - b))
        att = jnp.where(lower, att, 0.0)
        o_intra = jnp.einsum('bhij,bhje->bhie', att, vb)
        state = state * jnp.exp(b_end[:, :, 0])[..., None] + jnp.einsum('bhcd,bhce->bhde', kb * jnp.exp(b_end - b), vb)
        return state, o_inter + o_intra

    s_fin, o = lax.scan(step, s0, (blocks(q), blocks(k), blocks(v), blocks(log_a)))
    return o.transpose(1, 0, 3, 2, 4).reshape(B, L, H, DV), s_fin


def gla_bidir(q, k, v, la_f, la_b, s0_f, s0_b):
    o_f, s_f = gla_scan(q, k, v, la_f, s0_f)
    flip = lambda t: jnp.flip(t, axis=1)
    o_b, s_b = gla_scan(flip(q), flip(k), flip(v), flip(la_b), s0_b)
    return o_f + flip(o_b), s_f, s_b


def gla_output(o, pg, norm_g, w_o):
    B, L = o.shape[:2]
    on = rmsnorm(o, norm_g.reshape(GLA_HEADS, GLA_DV)).reshape(B, L, GLA_V)
    return (on.astype(pg.dtype) * jax.nn.silu(pg)) @ w_o


def na_latent(q, k, v, kc, vc, rpb):
    B, S, H, hd = q.shape
    rows = S // GRID_W
    kh = min(NA_KH, rows)
    kw, qc = NA_KW, NA_QC
    qr = math.gcd(rows, NA_QR)
    rr = min(qr + kh - 1, rows)
    cr = qc + kw - 1
    nrb, ncb = rows // qr, GRID_W // qc
    rs = np.clip(np.arange(rows) - kh // 2, 0, rows - kh)
    cs = np.clip(np.arange(GRID_W) - kw // 2, 0, GRID_W - kw)
    row_idx = np.minimum(rs[::qr], rows - rr)[:, None] + np.arange(rr)
    col_idx = np.minimum(cs[::qc], GRID_W - cr)[:, None] + np.arange(cr)
    q_rows = np.arange(rows).reshape(nrb, qr)
    q_cols = np.arange(GRID_W).reshape(ncb, qc)
    d_r = row_idx[:, None, :] - q_rows[:, :, None]
    d_c = col_idx[:, None, :] - q_cols[:, :, None]
    ok_r = (row_idx[:, None, :] >= rs[q_rows][:, :, None]) & (row_idx[:, None, :] < rs[q_rows][:, :, None] + kh)
    ok_c = (col_idx[:, None, :] >= cs[q_cols][:, :, None]) & (col_idx[:, None, :] < cs[q_cols][:, :, None] + kw)
    i_r = np.clip(d_r + NA_KH - 1, 0, 2 * NA_KH - 2)[:, :, :, None, None, None]
    i_c = np.clip(d_c + NA_KW - 1, 0, 2 * NA_KW - 2)[None, None, None]
    ok = ok_r[:, :, :, None, None, None] & ok_c[None, None, None]
    bias = jnp.where(ok, rpb.astype(F32)[:, i_r, i_c], -jnp.inf).transpose(0, 1, 4, 2, 5, 3, 6)
    n_lat = rr * cr

    def one_sample(args):
        qs, ks_, vs, kcs, vcs = args
        qg = qs.reshape(nrb, qr, ncb, qc, H, hd)
        kg = jnp.take(jnp.take(ks_.reshape(rows, GRID_W, H, hd), row_idx, axis=0), col_idx, axis=2)
        vg = jnp.take(jnp.take(vs.reshape(rows, GRID_W, H, hd), row_idx, axis=0), col_idx, axis=2)
        s_lat = jnp.einsum('xqyphd,xryshd->hxyqprs', qg, kg).astype(F32) + bias
        s_ctx = jnp.einsum('xqyphd,chd->hxyqpc', qg, kcs).astype(F32)
        prob = jax.nn.softmax(jnp.concatenate([s_lat.reshape(*s_lat.shape[:5], n_lat), s_ctx], axis=-1), axis=-1)
        p_lat = prob[..., :n_lat].reshape(s_lat.shape).astype(vs.dtype)
        p_ctx = prob[..., n_lat:].astype(vs.dtype)
        out = jnp.einsum('hxyqprs,xryshd->xqyphd', p_lat, vg) + jnp.einsum('hxyqpc,chd->xqyphd', p_ctx, vcs)
        return out.reshape(S, H * hd)

    return lax.map(one_sample, (q * NA_HD ** -0.5, k, v, kc, vc))


def na_context(qc, kc, vc):
    s = jnp.einsum('bihd,bjhd->bhij', qc * NA_HD ** -0.5, kc).astype(F32)
    p = jax.nn.softmax(s, axis=-1).astype(vc.dtype)
    o = jnp.einsum('bhij,bjhd->bihd', p, vc)
    return o.reshape(qc.shape[0], qc.shape[1], NA_DIM)


def merge_branches(pre_gate, b_gate, y_conv, y_gla, y_na, w_out):
    g = jax.nn.sigmoid((pre_gate + b_gate).astype(F32)).reshape(*pre_gate.shape[:-1], N_BRANCH, D_MODEL)
    m = g[..., 0, :] * y_conv + g[..., 1, :] * y_gla + g[..., 2, :] * y_na
    return m.astype(pre_gate.dtype) @ w_out


def token_mixer(h, hc, need_ctx, w_in, b_gate, conv_w, conv_b, conv_ln_g, conv_ln_b, w_conv_o,
                gla_wa2, gla_ba, gla_norm_g, w_gla_o, na_rpb, w_na_o, w_out):
    B, S, _ = h.shape
    Lc = hc.shape[1]
    l_glu, l_q, l_k, l_v, l_g, l_a, l_nq, l_nk, l_nv, l_br = split_cols(h @ w_in)
    c_glu, c_q, c_k, c_v, c_g, c_a, c_nq, c_nk, c_nv, c_br = split_cols(hc @ w_in)
    cq, ck, cv, cla_f, cla_b = gla_features(c_q, c_k, c_v, c_a, gla_wa2, gla_ba, None)
    zero = jnp.zeros((B, GLA_HEADS, GLA_DK, GLA_DV), F32)
    co_gla, s_f, s_b = gla_bidir(cq, ck, cv, cla_f, cla_b, zero, zero)
    q, k, v, la_f, la_b = gla_features(l_q, l_k, l_v, l_a, gla_wa2, gla_ba, axial_rope(S))
    o_gla, _, _ = gla_bidir(q, k, v, la_f, la_b, s_f, s_b)
    heads = lambda t, L: t.reshape(B, L, NA_HEADS, NA_HD)
    nkc, nvc = heads(c_nk, Lc), heads(c_nv, Lc)
    y_na = na_latent(heads(l_nq, S), heads(l_nk, S), heads(l_nv, S), nkc, nvc, na_rpb)
    y = merge_branches(l_br, b_gate,
                       conv_module(l_glu, conv_w, conv_b, conv_ln_g, conv_ln_b, w_conv_o),
                       gla_output(o_gla, l_g, gla_norm_g, w_gla_o),
                       y_na @ w_na_o, w_out)
    yc = None
    if need_ctx:
        yc = merge_branches(c_br, b_gate,
                            conv_module(c_glu, conv_w, conv_b, conv_ln_g, conv_ln_b, w_conv_o),
                            gla_output(co_gla, c_g, gla_norm_g, w_gla_o),
                            na_context(heads(c_nq, Lc), nkc, nvc) @ w_na_o, w_out)
    return y, yc


def peer_ffn(h, wq, sub_keys, u_tab, v_tab):
    B, L, D = h.shape
    T = B * L
    hf = h.reshape(T, D)
    qry = (hf @ wq).reshape(T, PEER_HEADS, 2, PEER_DKEY // 2)
    s = jnp.einsum('thzd,hzkd->thzk', qry, sub_keys).astype(F32)
    ts, ti = lax.top_k(s, PEER_TOPK)
    cand_s = (ts[:, :, 0, :, None] + ts[:, :, 1, None, :]).reshape(T, PEER_HEADS, PEER_TOPK * PEER_TOPK)
    cand_e = (ti[:, :, 0, :, None] * PEER_NKEYS + ti[:, :, 1, None, :]).reshape(T, PEER_HEADS, PEER_TOPK * PEER_TOPK)
    best_s, best_j = lax.top_k(cand_s, PEER_TOPK)
    experts = jnp.take_along_axis(cand_e, best_j, axis=-1)
    gates = jax.nn.softmax(best_s, axis=-1).astype(h.dtype)
    nb = T // PEER_CHUNK

    def block(args):
        hb, eb, gb = args
        act = jax.nn.gelu(jnp.einsum('td,thkd->thk', hb, jnp.take(u_tab, eb, axis=0)), approximate=False)
        return jnp.einsum('thk,thkd->td', gb * act, jnp.take(v_tab, eb, axis=0))

    out = lax.map(block, (hf.reshape(nb, PEER_CHUNK, D),
                          experts.reshape(nb, PEER_CHUNK, PEER_HEADS, PEER_TOPK),
                          gates.reshape(nb, PEER_CHUNK, PEER_HEADS, PEER_TOPK)))
    return out.reshape(B, L, D)


def _final_norm_kernel(x_ref, g_ref, o_ref):
    x = x_ref[...]
    y = x * lax.rsqrt(jnp.mean(x * x, axis=-1, keepdims=True) + EPS)
    o_ref[...] = y * g_ref[...]


def final_norm(x, g):
    T = x.shape[0]
    tm = 512
    return pl.pallas_call(
        _final_norm_kernel,
        grid=(T // tm,),
        in_specs=[pl.BlockSpec((tm, D_MODEL), lambda i: (i, 0)),
                  pl.BlockSpec((1, D_MODEL), lambda i: (0, 0))],
        out_specs=pl.BlockSpec((tm, D_MODEL), lambda i: (i, 0)),
        out_shape=jax.ShapeDtypeStruct((T, D_MODEL), F32),
        name="final_norm",
    )(x, g.reshape(1, D_MODEL))


def kernel(x, c, ctx, c_ctx, w_mod, b_mod, norm1_g, norm2_g, w_in, b_gate, conv_w, conv_b,
           conv_ln_g, conv_ln_b, w_conv_o, gla_wa2, gla_ba, gla_norm_g, w_gla_o, na_rpb, w_na_o,
           w_out, peer_wq, peer_keys, peer_u, peer_v, final_g):
    xc = ctx
    cond = jax.nn.silu(c.astype(F32))
    cond_ctx = jax.nn.silu(c_ctx.astype(F32))
    for i in range(DEPTH):
        last = i == DEPTH - 1
        mod = (cond @ w_mod[i].astype(F32) + b_mod[i].astype(F32)).astype(x.dtype)[:, None, :]
        modc = (cond_ctx @ w_mod[i].astype(F32) + b_mod[i].astype(F32)).astype(x.dtype)
        sh1, sc1, g1, sh2, sc2, g2 = jnp.split(mod, 6, axis=-1)
        csh1, csc1, cg1, csh2, csc2, cg2 = jnp.split(modc, 6, axis=-1)
        h = rmsnorm(x, norm1_g[i]) * (1 + sc1) + sh1
        hc = rmsnorm(xc, norm1_g[i]) * (1 + csc1) + csh1
        y, yc = token_mixer(h, hc, not last, w_in[i], b_gate[i], conv_w[i], conv_b[i], conv_ln_g[i],
                            conv_ln_b[i], w_conv_o[i], gla_wa2[i], gla_ba[i], gla_norm_g[i], w_gla_o[i],
                            na_rpb[i], w_na_o[i], w_out[i])
        x = x + g1 * y.astype(x.dtype)
        h2 = rmsnorm(x, norm2_g[i]) * (1 + sc2) + sh2
        x = x + g2 * peer_ffn(h2, peer_wq[i], peer_keys[i], peer_u[i], peer_v[i]).astype(x.dtype)
        if not last:
            xc = xc + cg1 * yc.astype(xc.dtype)
            hc2 = rmsnorm(xc, norm2_g[i]) * (1 + csc2) + csh2
            xc = xc + cg2 * peer_ffn(hc2, peer_wq[i], peer_keys[i], peer_u[i], peer_v[i]).astype(xc.dtype)
    return final_norm(x.reshape(BATCH * SEQ, D_MODEL), final_g).reshape(BATCH, SEQ, D_MODEL)
```

```python
import math
import functools
import jax
import jax.numpy as jnp
from jax import lax
import numpy as np
from jax.experimental import pallas as pl
from jax.experimental.pallas import tpu as pltpu

D_MODEL = 1024
BATCH = 8
SEQ = 4096
DEPTH = 2
GRID_W = 64
CTX_LEN = 256
EPS = 1e-6
N_BRANCH = 3
F32 = jnp.float32
BF16 = jnp.bfloat16
CONV_DIM = 512
CONV_WIDTH = 31
GLA_HEADS = 4
GLA_DK = 64
GLA_DV = 128
GLA_RANK = 16
GLA_GATE_NORM = 16.0
GLA_CHUNK = 64
GLA_QK = GLA_HEADS * GLA_DK
GLA_V = GLA_HEADS * GLA_DV
ROPE_BASE = 10000.0
NA_HEADS = 8
NA_HD = 64
NA_DIM = NA_HEADS * NA_HD
NA_KH = 8
NA_KW = 16
NA_QR = 8
NA_QC = 16
PEER_HEADS = 8
PEER_NKEYS = 128
PEER_EXPERTS = PEER_NKEYS * PEER_NKEYS
PEER_TOPK = 16
PEER_DKEY = 256
PEER_TT = 1024
PEER_EB = 1024
IN_SPLITS = (2 * CONV_DIM, GLA_QK, GLA_QK, GLA_V, GLA_V, 2 * GLA_RANK, NA_DIM, NA_DIM, NA_DIM, N_BRANCH * D_MODEL)
IN_DIM = sum(IN_SPLITS)


def rmsnorm(x, g):
    xf = x.astype(F32)
    y = xf * lax.rsqrt(jnp.mean(xf * xf, axis=-1, keepdims=True) + EPS)
    return (y * g.astype(F32)).astype(x.dtype)


def layernorm(x, g, b):
    xf = x.astype(F32)
    mu = jnp.mean(xf, axis=-1, keepdims=True)
    xc = xf - mu
    var = jnp.mean(xc * xc, axis=-1, keepdims=True)
    return (xc * lax.rsqrt(var + EPS) * g.astype(F32) + b.astype(F32)).astype(x.dtype)


def split_cols(p):
    return jnp.split(p, np.cumsum(IN_SPLITS)[:-1].tolist(), axis=-1)


def conv_module(glu_in, conv_w, conv_b, ln_g, ln_b, w_o):
    u = glu_in[..., :CONV_DIM] * jax.nn.sigmoid(glu_in[..., CONV_DIM:])
    u = lax.conv_general_dilated(
        u, conv_w.astype(u.dtype)[:, None, :], (1,), [(CONV_WIDTH // 2, CONV_WIDTH // 2)],
        dimension_numbers=('NWC', 'WIO', 'NWC'), feature_group_count=CONV_DIM) + conv_b.astype(u.dtype)
    u = jax.nn.silu(layernorm(u, ln_g, ln_b))
    return u @ w_o


def axial_rope(L):
    t = jnp.arange(L)
    row = (t // GRID_W).astype(F32)
    col = (t % GRID_W).astype(F32)
    n_freq = GLA_DK // 4
    inv = ROPE_BASE ** (-jnp.arange(n_freq, dtype=F32) / n_freq)
    ang = jnp.concatenate([row[:, None] * inv, col[:, None] * inv], axis=-1)
    return jnp.cos(ang), jnp.sin(ang)


def apply_rope(x, cos, sin):
    x1, x2 = x[..., 0::2], x[..., 1::2]
    c = cos[None, :, None, :]
    s = sin[None, :, None, :]
    return jnp.stack([x1 * c - x2 * s, x1 * s + x2 * c], axis=-1).reshape(x.shape)


def gla_features(pq, pk, pv, pa, wa2, ba, rope):
    B, L, _ = pq.shape
    q = pq.astype(F32).reshape(B, L, GLA_HEADS, GLA_DK) * GLA_DK ** -0.5
    k = pk.astype(F32).reshape(B, L, GLA_HEADS, GLA_DK)
    if rope is not None:
        q = apply_rope(q, *rope)
        k = apply_rope(k, *rope)
    v = pv.astype(F32).reshape(B, L, GLA_HEADS, GLA_DV)
    z = jnp.einsum('blzr,zrk->blzk', pa.astype(F32).reshape(B, L, 2, GLA_RANK), wa2.astype(F32)) + ba.astype(F32)
    log_a = (jax.nn.log_sigmoid(z) / GLA_GATE_NORM).reshape(B, L, 2, GLA_HEADS, GLA_DK)
    return q, k, v, log_a[:, :, 0], log_a[:, :, 1]


def gla_scan(q, k, v, log_a, s0):
    B, L, H, _ = q.shape
    DV = v.shape[-1]
    C = GLA_CHUNK
    n = L // C

    def blocks(t):
        return t.reshape(B, n, C, H, t.shape[-1]).transpose(1, 0, 3, 2, 4)

    lower = jnp.tril(jnp.ones((C, C), dtype=bool))

    def step(state, inp):
        qb, kb, vb, ab = inp
        b = jnp.cumsum(ab, axis=2)
        b_mid = b[:, :, C // 2:C // 2 + 1]
        b_end = b[:, :, C - 1:]
        o_inter = jnp.einsum('bhcd,bhde->bhce', qb * jnp.exp(b), state)
        att = jnp.einsum('bhid,bhjd->bhij', qb * jnp.exp(b - b_mid), kb * jnp.exp(b_mid - b))
        att = jnp.where(lower, att, 0.0)
        o_intra = jnp.einsum('bhij,bhje->bhie', att, vb)
        state = state * jnp.exp(b_end[:, :, 0])[..., None] + jnp.einsum('bhcd,bhce->bhde', kb * jnp.exp(b_end - b), vb)
        return state, o_inter + o_intra

    s_fin, o = lax.scan(step, s0, (blocks(q), blocks(k), blocks(v), blocks(log_a)))
    return o.transpose(1, 0, 3, 2, 4).reshape(B, L, H, DV), s_fin


def gla_bidir(q, k, v, la_f, la_b, s0_f, s0_b):
    o_f, s_f = gla_scan(q, k, v, la_f, s0_f)
    flip = lambda t: jnp.flip(t, axis=1)
    o_b, s_b = gla_scan(flip(q), flip(k), flip(v), flip(la_b), s0_b)
    return o_f + flip(o_b), s_f, s_b


def gla_output(o, pg, norm_g, w_o):
    B, L = o.shape[:2]
    on = rmsnorm(o, norm_g.reshape(GLA_HEADS, GLA_DV)).reshape(B, L, GLA_V)
    return (on.astype(pg.dtype) * jax.nn.silu(pg)) @ w_o


def na_latent(q, k, v, kc, vc, rpb):
    B, S, H, hd = q.shape
    rows = S // GRID_W
    kh = min(NA_KH, rows)
    kw, qc = NA_KW, NA_QC
    qr = math.gcd(rows, NA_QR)
    rr = min(qr + kh - 1, rows)
    cr = qc + kw - 1
    nrb, ncb = rows // qr, GRID_W // qc
    rs = np.clip(np.arange(rows) - kh // 2, 0, rows - kh)
    cs = np.clip(np.arange(GRID_W) - kw // 2, 0, GRID_W - kw)
    row_idx = np.minimum(rs[::qr], rows - rr)[:, None] + np.arange(rr)
    col_idx = np.minimum(cs[::qc], GRID_W - cr)[:, None] + np.arange(cr)
    q_rows = np.arange(rows).reshape(nrb, qr)
    q_cols = np.arange(GRID_W).reshape(ncb, qc)
    d_r = row_idx[:, None, :] - q_rows[:, :, None]
    d_c = col_idx[:, None, :] - q_cols[:, :, None]
    ok_r = (row_idx[:, None, :] >= rs[q_rows][:, :, None]) & (row_idx[:, None, :] < rs[q_rows][:, :, None] + kh)
    ok_c = (col_idx[:, None, :] >= cs[q_cols][:, :, None]) & (col_idx[:, None, :] < cs[q_cols][:, :, None] + kw)
    i_r = np.clip(d_r + NA_KH - 1, 0, 2 * NA_KH - 2)[:, :, :, None, None, None]
    i_c = np.clip(d_c + NA_KW - 1, 0, 2 * NA_KW - 2)[None, None, None]
    ok = ok_r[:, :, :, None, None, None] & ok_c[None, None, None]
    bias = jnp.where(ok, rpb.astype(F32)[:, i_r, i_c], -jnp.inf).transpose(0, 1, 4, 2, 5, 3, 6)
    n_lat = rr * cr

    def one_sample(args):
        qs, ks_, vs, kcs, vcs = args
        qg = qs.reshape(nrb, qr, ncb, qc, H, hd)
        kg = jnp.take(jnp.take(ks_.reshape(rows, GRID_W, H, hd), row_idx, axis=0), col_idx, axis=2)
        vg = jnp.take(jnp.take(vs.reshape(rows, GRID_W, H, hd), row_idx, axis=0), col_idx, axis=2)
        s_lat = jnp.einsum('xqyphd,xryshd->hxyqprs', qg, kg).astype(F32) + bias
        s_ctx = jnp.einsum('xqyphd,chd->hxyqpc', qg, kcs).astype(F32)
        prob = jax.nn.softmax(jnp.concatenate([s_lat.reshape(*s_lat.shape[:5], n_lat), s_ctx], axis=-1), axis=-1)
        p_lat = prob[..., :n_lat].reshape(s_lat.shape).astype(vs.dtype)
        p_ctx = prob[..., n_lat:].astype(vs.dtype)
        out = jnp.einsum('hxyqprs,xryshd->xqyphd', p_lat, vg) + jnp.einsum('hxyqpc,chd->xqyphd', p_ctx, vcs)
        return out.reshape(S, H * hd)

    return lax.map(one_sample, (q * NA_HD ** -0.5, k, v, kc, vc))


def na_context(qc, kc, vc):
    s = jnp.einsum('bihd,bjhd->bhij', qc * NA_HD ** -0.5, kc).astype(F32)
    p = jax.nn.softmax(s, axis=-1).astype(vc.dtype)
    o = jnp.einsum('bhij,bjhd->bihd', p, vc)
    return o.reshape(qc.shape[0], qc.shape[1], NA_DIM)


def merge_branches(pre_gate, b_gate, y_conv, y_gla, y_na, w_out):
    g = jax.nn.sigmoid((pre_gate + b_gate).astype(F32)).reshape(*pre_gate.shape[:-1], N_BRANCH, D_MODEL)
    m = g[..., 0, :] * y_conv + g[..., 1, :] * y_gla + g[..., 2, :] * y_na
    return m.astype(pre_gate.dtype) @ w_out


def token_mixer(h, hc, need_ctx, w_in, b_gate, conv_w, conv_b, conv_ln_g, conv_ln_b, w_conv_o,
                gla_wa2, gla_ba, gla_norm_g, w_gla_o, na_rpb, w_na_o, w_out):
    B, S, _ = h.shape
    Lc = hc.shape[1]
    l_glu, l_q, l_k, l_v, l_g, l_a, l_nq, l_nk, l_nv, l_br = split_cols(h @ w_in)
    c_glu, c_q, c_k, c_v, c_g, c_a, c_nq, c_nk, c_nv, c_br = split_cols(hc @ w_in)
    cq, ck, cv, cla_f, cla_b = gla_features(c_q, c_k, c_v, c_a, gla_wa2, gla_ba, None)
    zero = jnp.zeros((B, GLA_HEADS, GLA_DK, GLA_DV), F32)
    co_gla, s_f, s_b = gla_bidir(cq, ck, cv, cla_f, cla_b, zero, zero)
    q, k, v, la_f, la_b = gla_features(l_q, l_k, l_v, l_a, gla_wa2, gla_ba, axial_rope(S))
    o_gla, _, _ = gla_bidir(q, k, v, la_f, la_b, s_f, s_b)
    heads = lambda t, L: t.reshape(B, L, NA_HEADS, NA_HD)
    nkc, nvc = heads(c_nk, Lc), heads(c_nv, Lc)
    y_na = na_latent(heads(l_nq, S), heads(l_nk, S), heads(l_nv, S), nkc, nvc, na_rpb)
    y = merge_branches(l_br, b_gate,
                       conv_module(l_glu, conv_w, conv_b, conv_ln_g, conv_ln_b, w_conv_o),
                       gla_output(o_gla, l_g, gla_norm_g, w_gla_o),
                       y_na @ w_na_o, w_out)
    yc = None
    if need_ctx:
        yc = merge_branches(c_br, b_gate,
                            conv_module(c_glu, conv_w, conv_b, conv_ln_g, conv_ln_b, w_conv_o),
                            gla_output(co_gla, c_g, gla_norm_g, w_gla_o),
                            na_context(heads(c_nq, Lc), nkc, nvc) @ w_na_o, w_out)
    return y, yc


LANES = 128
SUBLANES = 8
NT_DIMS = (((1,), (1,)), ((), ()))
VMEM_LIMIT = 56 * 1024 * 1024


def _peer_route_kernel(h_ref, wq_ref, keys_ref, st_ref, stats_ref, hb_ref, *, nh, topk):
    tt = h_ref.shape[0]
    hb = h_ref[...].astype(BF16)
    hb_ref[...] = hb
    qry = jnp.dot(hb, wq_ref[...], preferred_element_type=F32)
    neg = jnp.float32(-jnp.inf)
    for h in range(nh):
        tops = []
        for z in range(2):
            hz = 2 * h + z
            q = qry[:, hz * LANES:(hz + 1) * LANES].astype(BF16)
            s = lax.dot_general(keys_ref[hz], q, NT_DIMS, preferred_element_type=F32)
            st_ref[hz] = s
            cur = s
            vals = []
            for _ in range(topk):
                m = jnp.max(cur, axis=0, keepdims=True)
                vals.append(m)
                cur = jnp.where(cur == m, neg, cur)
            tops.append(vals)
        a, b = tops
        cands = [a[p] + b[q] for p in range(topk) for q in range(topk) if (p + 1) * (q + 1) <= topk]
        npad = (-len(cands)) % 8
        cand = jnp.concatenate(cands + [jnp.full((npad, tt), neg, F32)], axis=0)
        cur = cand
        tau = None
        for _ in range(topk):
            tau = jnp.max(cur, axis=0, keepdims=True)
            cur = jnp.where(cur == tau, neg, cur)
        top = a[0] + b[0]
        zsum = jnp.sum(jnp.where(cand >= tau, jnp.exp(cand - top), 0.0), axis=0, keepdims=True)
        stats_ref[h] = jnp.concatenate([tau, a[0], b[0], 1.0 / zsum, jnp.zeros((4, tt), F32)], axis=0)


def _peer_main_kernel(hb_ref, st_ref, stats_ref, u_ref, vt_ref, out_ref, e0_ref, e1_ref, act_ref, w_ref,
                      *, nh, eb, tt):
    i = pl.program_id(1)

    @pl.when(i == 0)
    def _():
        out_ref[...] = jnp.zeros(out_ref.shape, F32)
        for h in range(nh):
            a1 = stats_ref[h, 1:2, :]
            b1 = stats_ref[h, 2:3, :]
            rz = stats_ref[h, 3:4, :]
            e0_ref[h] = jnp.exp(st_ref[2 * h] - a1) * rz
            e1_ref[h] = jnp.exp(st_ref[2 * h + 1] - b1)

    act_ref[...] = lax.dot_general(u_ref[...], hb_ref[...], NT_DIMS, preferred_element_type=F32)
    nsub = eb // LANES
    assert nsub == SUBLANES
    first = pl.ds(pl.multiple_of(i * SUBLANES, SUBLANES), SUBLANES)

    def lane_chunk(c, carry):
        cs = pl.ds(pl.multiple_of(c * LANES, LANES), LANES)
        for s in range(nsub):
            rs = slice(s * LANES, (s + 1) * LANES)
            g = jnp.zeros((LANES, LANES), F32)
            for h in range(nh):
                s0 = st_ref[2 * h, first, cs][s:s + 1]
                e0 = e0_ref[h, first, cs][s:s + 1]
                tau = stats_ref[h, 0:1, cs]
                s1 = st_ref[2 * h + 1, :, cs]
                e1 = e1_ref[h, :, cs]
                g = g + jnp.where(s0 + s1 >= tau, e0 * e1, 0.0)
            a = act_ref[rs, cs]
            ge = 0.5 * a * (1.0 + lax.erf(a * np.float32(math.sqrt(0.5))))
            w_ref[rs, cs] = (g * ge).astype(BF16)
        return carry

    lax.fori_loop(0, tt // LANES, lane_chunk, 0)
    out_ref[...] += jnp.dot(vt_ref[...], w_ref[...], preferred_element_type=F32)


def peer_pallas(h2, wq, keys, u_tab, v_tab, *, tt, eb, topk):
    T, D = h2.shape
    nh = keys.shape[0]
    nk = keys.shape[2]
    assert nk == LANES and keys.shape[3] == LANES and T % tt == 0
    E = nk * nk
    wq_b = wq.astype(BF16)
    keys_b = keys.reshape(2 * nh, nk, LANES).astype(BF16)
    nt = T // tt
    st, stats, hb = pl.pallas_call(
        functools.partial(_peer_route_kernel, nh=nh, topk=topk),
        grid=(nt,),
        in_specs=[pl.BlockSpec((tt, D), lambda t: (t, 0)),
                  pl.BlockSpec((D, 2 * nh * LANES), lambda t: (0, 0)),
                  pl.BlockSpec((2 * nh, nk, LANES), lambda t: (0, 0, 0))],
        out_specs=[pl.BlockSpec((2 * nh, nk, tt), lambda t: (0, 0, t)),
                   pl.BlockSpec((nh, 8, tt), lambda t: (0, 0, t)),
                   pl.BlockSpec((tt, D), lambda t: (t, 0))],
        out_shape=[jax.ShapeDtypeStruct((2 * nh, nk, T), F32),
                   jax.ShapeDtypeStruct((nh, 8, T), F32),
                   jax.ShapeDtypeStruct((T, D), BF16)],
        compiler_params=pltpu.CompilerParams(dimension_semantics=("parallel",),
                                             vmem_limit_bytes=VMEM_LIMIT),
        name="peer_route",
    )(h2, wq_b, keys_b)
    u_b = u_tab.astype(BF16)
    vt_b = v_tab.T.astype(BF16)
    out_t = pl.pallas_call(
        functools.partial(_peer_main_kernel, nh=nh, eb=eb, tt=tt),
        grid=(nt, E // eb),
        in_specs=[pl.BlockSpec((tt, D), lambda t, i: (t, 0)),
                  pl.BlockSpec((2 * nh, nk, tt), lambda t, i: (0, 0, t)),
                  pl.BlockSpec((nh, 8, tt), lambda t, i: (0, 0, t)),
                  pl.BlockSpec((eb, D), lambda t, i: (i, 0)),
                  pl.BlockSpec((D, eb), lambda t, i: (0, i))],
        out_specs=pl.BlockSpec((D, tt), lambda t, i: (0, t)),
        out_shape=jax.ShapeDtypeStruct((D, T), F32),
        scratch_shapes=[pltpu.VMEM((nh, nk, tt), F32),
                        pltpu.VMEM((nh, nk, tt), F32),
                        pltpu.VMEM((eb, tt), F32),
                        pltpu.VMEM((eb, tt), BF16)],
        compiler_params=pltpu.CompilerParams(dimension_semantics=("parallel", "arbitrary"),
                                             vmem_limit_bytes=VMEM_LIMIT),
        name="peer_main",
    )(hb, st, stats, u_b, vt_b)
    return out_t.T


def peer_tokens(hf, wq, sub_keys, u_tab, v_tab):
    return peer_pallas(hf, wq, sub_keys, u_tab, v_tab, tt=PEER_TT, eb=PEER_EB, topk=PEER_TOPK)


def _final_norm_kernel(x_ref, g_ref, o_ref):
    x = x_ref[...]
    y = x * lax.rsqrt(jnp.mean(x * x, axis=-1, keepdims=True) + EPS)
    o_ref[...] = y * g_ref[...]


def final_norm(x, g):
    T = x.shape[0]
    tm = 512
    return pl.pallas_call(
        _final_norm_kernel,
        grid=(T // tm,),
        in_specs=[pl.BlockSpec((tm, D_MODEL), lambda i: (i, 0)),
                  pl.BlockSpec((1, D_MODEL), lambda i: (0, 0))],
        out_specs=pl.BlockSpec((tm, D_MODEL), lambda i: (i, 0)),
        out_shape=jax.ShapeDtypeStruct((T, D_MODEL), F32),
        name="final_norm",
    )(x, g.reshape(1, D_MODEL))


def kernel(x, c, ctx, c_ctx, w_mod, b_mod, norm1_g, norm2_g, w_in, b_gate, conv_w, conv_b,
           conv_ln_g, conv_ln_b, w_conv_o, gla_wa2, gla_ba, gla_norm_g, w_gla_o, na_rpb, w_na_o,
           w_out, peer_wq, peer_keys, peer_u, peer_v, final_g):
    xc = ctx
    cond = jax.nn.silu(c.astype(F32))
    cond_ctx = jax.nn.silu(c_ctx.astype(F32))
    for i in range(DEPTH):
        last = i == DEPTH - 1
        mod = (cond @ w_mod[i].astype(F32) + b_mod[i].astype(F32)).astype(x.dtype)[:, None, :]
        modc = (cond_ctx @ w_mod[i].astype(F32) + b_mod[i].astype(F32)).astype(x.dtype)
        sh1, sc1, g1, sh2, sc2, g2 = jnp.split(mod, 6, axis=-1)
        csh1, csc1, cg1, csh2, csc2, cg2 = jnp.split(modc, 6, axis=-1)
        h = rmsnorm(x, norm1_g[i]) * (1 + sc1) + sh1
        hc = rmsnorm(xc, norm1_g[i]) * (1 + csc1) + csh1
        y, yc = token_mixer(h, hc, not last, w_in[i], b_gate[i], conv_w[i], conv_b[i], conv_ln_g[i],
                            conv_ln_b[i], w_conv_o[i], gla_wa2[i], gla_ba[i], gla_norm_g[i], w_gla_o[i],
                            na_rpb[i], w_na_o[i], w_out[i])
        x = x + g1 * y.astype(x.dtype)
        h2 = rmsnorm(x, norm2_g[i]) * (1 + sc2) + sh2
        toks = h2.reshape(BATCH * SEQ, D_MODEL)
        if not last:
            xc = xc + cg1 * yc.astype(xc.dtype)
            hc2 = rmsnorm(xc, norm2_g[i]) * (1 + csc2) + csh2
            toks = jnp.concatenate([toks, hc2.reshape(BATCH * CTX_LEN, D_MODEL)], axis=0)
        p = peer_tokens(toks, peer_wq[i], peer_keys[i], peer_u[i], peer_v[i])
        x = x + g2 * p[:BATCH * SEQ].reshape(BATCH, SEQ, D_MODEL)
        if not last:
            xc = xc + cg2 * p[BATCH * SEQ:].reshape(BATCH, CTX_LEN, D_MODEL)
    return final_norm(x.reshape(BATCH * SEQ, D_MODEL), final_g).reshape(BATCH, SEQ, D_MODEL)
```

```python
import math
import functools
import jax
import jax.numpy as jnp
from jax import lax
import numpy as np
from jax.experimental import pallas as pl
from jax.experimental.pallas import tpu as pltpu

D_MODEL = 1024
BATCH = 8
SEQ = 4096
DEPTH = 2
GRID_W = 64
CTX_LEN = 256
EPS = 1e-6
N_BRANCH = 3
F32 = jnp.float32
BF16 = jnp.bfloat16
CONV_DIM = 512
CONV_WIDTH = 31
GLA_HEADS = 4
GLA_DK = 64
GLA_DV = 128
GLA_RANK = 16
GLA_GATE_NORM = 16.0
GLA_CHUNK = 64
GLA_QK = GLA_HEADS * GLA_DK
GLA_V = GLA_HEADS * GLA_DV
ROPE_BASE = 10000.0
NA_HEADS = 8
NA_HD = 64
NA_DIM = NA_HEADS * NA_HD
NA_KH = 8
NA_KW = 16
NA_QR = 8
NA_QC = 16
PEER_HEADS = 8
PEER_NKEYS = 128
PEER_EXPERTS = PEER_NKEYS * PEER_NKEYS
PEER_TOPK = 16
PEER_DKEY = 256
PEER_TT = 1024
PEER_EB = 1024
IN_SPLITS = (2 * CONV_DIM, GLA_QK, GLA_QK, GLA_V, GLA_V, 2 * GLA_RANK, NA_DIM, NA_DIM, NA_DIM, N_BRANCH * D_MODEL)
IN_DIM = sum(IN_SPLITS)


def rmsnorm(x, g):
    xf = x.astype(F32)
    y = xf * lax.rsqrt(jnp.mean(xf * xf, axis=-1, keepdims=True) + EPS)
    return (y * g.astype(F32)).astype(x.dtype)


def layernorm(x, g, b):
    xf = x.astype(F32)
    mu = jnp.mean(xf, axis=-1, keepdims=True)
    xc = xf - mu
    var = jnp.mean(xc * xc, axis=-1, keepdims=True)
    return (xc * lax.rsqrt(var + EPS) * g.astype(F32) + b.astype(F32)).astype(x.dtype)


def split_cols(p):
    return jnp.split(p, np.cumsum(IN_SPLITS)[:-1].tolist(), axis=-1)


def conv_module(glu_in, conv_w, conv_b, ln_g, ln_b, w_o):
    u = glu_in[..., :CONV_DIM] * jax.nn.sigmoid(glu_in[..., CONV_DIM:])
    u = lax.conv_general_dilated(
        u, conv_w.astype(u.dtype)[:, None, :], (1,), [(CONV_WIDTH // 2, CONV_WIDTH // 2)],
        dimension_numbers=('NWC', 'WIO', 'NWC'), feature_group_count=CONV_DIM) + conv_b.astype(u.dtype)
    u = jax.nn.silu(layernorm(u, ln_g, ln_b))
    return u @ w_o


def axial_rope(L):
    t = jnp.arange(L)
    row = (t // GRID_W).astype(F32)
    col = (t % GRID_W).astype(F32)
    n_freq = GLA_DK // 4
    inv = ROPE_BASE ** (-jnp.arange(n_freq, dtype=F32) / n_freq)
    ang = jnp.concatenate([row[:, None] * inv, col[:, None] * inv], axis=-1)
    return jnp.cos(ang), jnp.sin(ang)


def apply_rope(x, cos, sin):
    x1, x2 = x[..., 0::2], x[..., 1::2]
    c = cos[None, :, None, :]
    s = sin[None, :, None, :]
    return jnp.stack([x1 * c - x2 * s, x1 * s + x2 * c], axis=-1).reshape(x.shape)


def gla_features(pq, pk, pv, pa, wa2, ba, rope):
    B, L, _ = pq.shape
    q = pq.astype(F32).reshape(B, L, GLA_HEADS, GLA_DK) * GLA_DK ** -0.5
    k = pk.astype(F32).reshape(B, L, GLA_HEADS, GLA_DK)
    if rope is not None:
        q = apply_rope(q, *rope)
        k = apply_rope(k, *rope)
    v = pv.astype(F32).reshape(B, L, GLA_HEADS, GLA_DV)
    z = jnp.einsum('blzr,zrk->blzk', pa.astype(F32).reshape(B, L, 2, GLA_RANK), wa2.astype(F32)) + ba.astype(F32)
    log_a = (jax.nn.log_sigmoid(z) / GLA_GATE_NORM).reshape(B, L, 2, GLA_HEADS, GLA_DK)
    return q, k, v, log_a[:, :, 0], log_a[:, :, 1]


def gla_scan(q, k, v, log_a, s0):
    B, L, H, _ = q.shape
    DV = v.shape[-1]
    C = GLA_CHUNK
    n = L // C

    def blocks(t):
        return t.reshape(B, n, C, H, t.shape[-1]).transpose(1, 0, 3, 2, 4)

    lower = jnp.tril(jnp.ones((C, C), dtype=bool))

    def step(state, inp):
        qb, kb, vb, ab = inp
        b = jnp.cumsum(ab, axis=2)
        b_mid = b[:, :, C // 2:C // 2 + 1]
        b_end = b[:, :, C - 1:]
        o_inter = jnp.einsum('bhcd,bhde->bhce', qb * jnp.exp(b), state)
        att = jnp.einsum('bhid,bhjd->bhij', qb * jnp.exp(b - b_mid), kb * jnp.exp(b_mid - b))
        att = jnp.where(lower, att, 0.0)
        o_intra = jnp.einsum('bhij,bhje->bhie', att, vb)
        state = state * jnp.exp(b_end[:, :, 0])[..., None] + jnp.einsum('bhcd,bhce->bhde', kb * jnp.exp(b_end - b), vb)
        return state, o_inter + o_intra

    s_fin, o = lax.scan(step, s0, (blocks(q), blocks(k), blocks(v), blocks(log_a)))
    return o.transpose(1, 0, 3, 2, 4).reshape(B, L, H, DV), s_fin


def gla_bidir(q, k, v, la_f, la_b, s0_f, s0_b):
    o_f, s_f = gla_scan(q, k, v, la_f, s0_f)
    flip = lambda t: jnp.flip(t, axis=1)
    o_b, s_b = gla_scan(flip(q), flip(k), flip(v), flip(la_b), s0_b)
    return o_f + flip(o_b), s_f, s_b


def gla_output(o, pg, norm_g, w_o):
    B, L = o.shape[:2]
    on = rmsnorm(o, norm_g.reshape(GLA_HEADS, GLA_DV)).reshape(B, L, GLA_V)
    return (on.astype(pg.dtype) * jax.nn.silu(pg)) @ w_o


LANES = 128
SUBLANES = 8
NT_DIMS = (((1,), (1,)), ((), ()))
VMEM_LIMIT = 56 * 1024 * 1024
NA_QROWS = 8
NA_KROWS = 16
NA_QCHUNK = 256


def na_key_row_start(x, rows):
    return min(max(NA_QROWS * x - NA_KH // 2, 0), rows - NA_KROWS)


def na_bias_blocks(rpb, rows):
    W = GRID_W
    rs = np.clip(np.arange(rows) - NA_KH // 2, 0, rows - NA_KH)
    cs = np.clip(np.arange(W) - NA_KW // 2, 0, W - NA_KW)
    nrb = rows // NA_QROWS
    blocks = (0, 1, nrb - 1)
    qr = np.stack([np.arange(NA_QROWS) + NA_QROWS * x for x in blocks])
    kr = np.stack([na_key_row_start(x, rows) + np.arange(NA_KROWS) for x in blocks])
    ok_r = (kr[:, None, :] >= rs[qr][:, :, None]) & (kr[:, None, :] < rs[qr][:, :, None] + NA_KH)
    i_r = np.clip(kr[:, None, :] - qr[:, :, None] + NA_KH - 1, 0, 2 * NA_KH - 2)
    qc = np.arange(W)
    kc = np.arange(W)
    ok_c = (kc[None, :] >= cs[qc][:, None]) & (kc[None, :] < cs[qc][:, None] + NA_KW)
    i_c = np.clip(kc[None, :] - qc[:, None] + NA_KW - 1, 0, 2 * NA_KW - 2)
    ok = ok_r[:, :, None, :, None] & ok_c[None, None, :, None, :]
    vals = rpb.astype(F32)[:, i_r[:, :, None, :, None], i_c[None, None, :, None, :]]
    bias = jnp.where(ok[None], vals, -jnp.inf)
    return bias.reshape(rpb.shape[0], 3, NA_QROWS * W, NA_KROWS * W)


def _na_kernel(q_ref, k_ref, v_ref, kc_ref, vc_ref, bias_ref, o_ref, *, rows):
    x = pl.program_id(2)
    W = GRID_W
    start_row = jnp.minimum(jnp.maximum(NA_QROWS * x - NA_KH // 2, 0), rows - NA_KROWS)
    win = pl.ds(pl.multiple_of(start_row * W, W), NA_KROWS * W)
    kw = k_ref[0, win, :].astype(BF16)
    vw = v_ref[0, win, :].astype(BF16)
    kc = kc_ref[0].astype(BF16)
    vc = vc_ref[0].astype(BF16)
    lane = lax.broadcasted_iota(jnp.int32, (NA_QCHUNK, LANES), 1)
    scale = np.float32(NA_HD ** -0.5)
    nq = NA_QROWS * W
    for c in range(nq // NA_QCHUNK):
        qs = slice(c * NA_QCHUNK, (c + 1) * NA_QCHUNK)
        q = q_ref[0, qs, :] * scale
        outs = []
        for hh in range(2):
            in_head = (lane >= hh * NA_HD) & (lane < (hh + 1) * NA_HD)
            qm = jnp.where(in_head, q, 0.0).astype(BF16)
            s = lax.dot_general(qm, kw, NT_DIMS, preferred_element_type=F32) + bias_ref[hh, 0, qs, :]
            sc = lax.dot_general(qm, kc, NT_DIMS, preferred_element_type=F32)
            m = jnp.maximum(jnp.max(s, axis=-1, keepdims=True), jnp.max(sc, axis=-1, keepdims=True))
            p = jnp.exp(s - m)
            pc = jnp.exp(sc - m)
            l = jnp.sum(p, axis=-1, keepdims=True) + jnp.sum(pc, axis=-1, keepdims=True)
            o = (jnp.dot(p.astype(BF16), vw, preferred_element_type=F32)
                 + jnp.dot(pc.astype(BF16), vc, preferred_element_type=F32))
            outs.append(o / l)
        o_ref[0, qs, :] = jnp.where(lane < NA_HD, outs[0], outs[1])


def na_latent(q, k, v, kc, vc, rpb):
    B, S, HD = q.shape
    Lc = kc.shape[1]
    rows = S // GRID_W
    nrb = rows // NA_QROWS
    npair = HD // LANES
    bias = na_bias_blocks(rpb, rows)
    nq = NA_QROWS * GRID_W
    nk = NA_KROWS * GRID_W

    def bias_idx(p, b, x):
        return (p, jnp.where(x == 0, 0, jnp.where(x == nrb - 1, 2, 1)), 0, 0)

    return pl.pallas_call(
        functools.partial(_na_kernel, rows=rows),
        grid=(npair, B, nrb),
        in_specs=[pl.BlockSpec((1, nq, LANES), lambda p, b, x: (b, x, p)),
                  pl.BlockSpec((1, S, LANES), lambda p, b, x: (b, 0, p)),
                  pl.BlockSpec((1, S, LANES), lambda p, b, x: (b, 0, p)),
                  pl.BlockSpec((1, Lc, LANES), lambda p, b, x: (b, 0, p)),
                  pl.BlockSpec((1, Lc, LANES), lambda p, b, x: (b, 0, p)),
                  pl.BlockSpec((2, 1, nq, nk), bias_idx)],
        out_specs=pl.BlockSpec((1, nq, LANES), lambda p, b, x: (b, x, p)),
        out_shape=jax.ShapeDtypeStruct((B, S, HD), F32),
        compiler_params=pltpu.CompilerParams(dimension_semantics=("parallel", "parallel", "arbitrary"),
                                             vmem_limit_bytes=VMEM_LIMIT),
        name="na_latent",
    )(q, k, v, kc, vc, bias)


def na_context(qc, kc, vc):
    s = jnp.einsum('bihd,bjhd->bhij', qc * NA_HD ** -0.5, kc).astype(F32)
    p = jax.nn.softmax(s, axis=-1).astype(vc.dtype)
    o = jnp.einsum('bhij,bjhd->bihd', p, vc)
    return o.reshape(qc.shape[0], qc.shape[1], NA_DIM)


def merge_branches(pre_gate, b_gate, y_conv, y_gla, y_na, w_out):
    g = jax.nn.sigmoid((pre_gate + b_gate).astype(F32)).reshape(*pre_gate.shape[:-1], N_BRANCH, D_MODEL)
    m = g[..., 0, :] * y_conv + g[..., 1, :] * y_gla + g[..., 2, :] * y_na
    return m.astype(pre_gate.dtype) @ w_out


def token_mixer(h, hc, need_ctx, w_in, b_gate, conv_w, conv_b, conv_ln_g, conv_ln_b, w_conv_o,
                gla_wa2, gla_ba, gla_norm_g, w_gla_o, na_rpb, w_na_o, w_out):
    B, S, _ = h.shape
    Lc = hc.shape[1]
    l_glu, l_q, l_k, l_v, l_g, l_a, l_nq, l_nk, l_nv, l_br = split_cols(h @ w_in)
    c_glu, c_q, c_k, c_v, c_g, c_a, c_nq, c_nk, c_nv, c_br = split_cols(hc @ w_in)
    cq, ck, cv, cla_f, cla_b = gla_features(c_q, c_k, c_v, c_a, gla_wa2, gla_ba, None)
    zero = jnp.zeros((B, GLA_HEADS, GLA_DK, GLA_DV), F32)
    co_gla, s_f, s_b = gla_bidir(cq, ck, cv, cla_f, cla_b, zero, zero)
    q, k, v, la_f, la_b = gla_features(l_q, l_k, l_v, l_a, gla_wa2, gla_ba, axial_rope(S))
    o_gla, _, _ = gla_bidir(q, k, v, la_f, la_b, s_f, s_b)
    heads = lambda t, L: t.reshape(B, L, NA_HEADS, NA_HD)
    nkc, nvc = heads(c_nk, Lc), heads(c_nv, Lc)
    y_na = na_latent(l_nq, l_nk, l_nv, c_nk, c_nv, na_rpb)
    y = merge_branches(l_br, b_gate,
                       conv_module(l_glu, conv_w, conv_b, conv_ln_g, conv_ln_b, w_conv_o),
                       gla_output(o_gla, l_g, gla_norm_g, w_gla_o),
                       y_na @ w_na_o, w_out)
    yc = None
    if need_ctx:
        yc = merge_branches(c_br, b_gate,
                            conv_module(c_glu, conv_w, conv_b, conv_ln_g, conv_ln_b, w_conv_o),
                            gla_output(co_gla, c_g, gla_norm_g, w_gla_o),
                            na_context(heads(c_nq, Lc), nkc, nvc) @ w_na_o, w_out)
    return y, yc


def _peer_route_kernel(h_ref, wq_ref, keys_ref, st_ref, stats_ref, hb_ref, *, nh, topk):
    tt = h_ref.shape[0]
    hb = h_ref[...].astype(BF16)
    hb_ref[...] = hb
    qry = jnp.dot(hb, wq_ref[...], preferred_element_type=F32)
    neg = jnp.float32(-jnp.inf)
    for h in range(nh):
        tops = []
        for z in range(2):
            hz = 2 * h + z
            q = qry[:, hz * LANES:(hz + 1) * LANES].astype(BF16)
            s = lax.dot_general(keys_ref[hz], q, NT_DIMS, preferred_element_type=F32)
            st_ref[hz] = s
            cur = s
            vals = []
            for _ in range(topk):
                m = jnp.max(cur, axis=0, keepdims=True)
                vals.append(m)
                cur = jnp.where(cur == m, neg, cur)
            tops.append(vals)
        a, b = tops
        cands = [a[p] + b[q] for p in range(topk) for q in range(topk) if (p + 1) * (q + 1) <= topk]
        npad = (-len(cands)) % 8
        cand = jnp.concatenate(cands + [jnp.full((npad, tt), neg, F32)], axis=0)
        cur = cand
        tau = None
        for _ in range(topk):
            tau = jnp.max(cur, axis=0, keepdims=True)
            cur = jnp.where(cur == tau, neg, cur)
        top = a[0] + b[0]
        zsum = jnp.sum(jnp.where(cand >= tau, jnp.exp(cand - top), 0.0), axis=0, keepdims=True)
        stats_ref[h] = jnp.concatenate([tau, a[0], b[0], 1.0 / zsum, jnp.zeros((4, tt), F32)], axis=0)


def _peer_main_kernel(hb_ref, st_ref, stats_ref, u_ref, vt_ref, out_ref, e0_ref, e1_ref, act_ref, w_ref,
                      *, nh, eb, tt):
    i = pl.program_id(1)

    @pl.when(i == 0)
    def _():
        out_ref[...] = jnp.zeros(out_ref.shape, F32)
        for h in range(nh):
            a1 = stats_ref[h, 1:2, :]
            b1 = stats_ref[h, 2:3, :]
            rz = stats_ref[h, 3:4, :]
            e0_ref[h] = jnp.exp(st_ref[2 * h] - a1) * rz
            e1_ref[h] = jnp.exp(st_ref[2 * h + 1] - b1)

    act_ref[...] = lax.dot_general(u_ref[...], hb_ref[...], NT_DIMS, preferred_element_type=F32)
    nsub = eb // LANES
    assert nsub == SUBLANES
    first = pl.ds(pl.multiple_of(i * SUBLANES, SUBLANES), SUBLANES)

    def lane_chunk(c, carry):
        cs = pl.ds(pl.multiple_of(c * LANES, LANES), LANES)
        for s in range(nsub):
            rs = slice(s * LANES, (s + 1) * LANES)
            g = jnp.zeros((LANES, LANES), F32)
            for h in range(nh):
                s0 = st_ref[2 * h, first, cs][s:s + 1]
                e0 = e0_ref[h, first, cs][s:s + 1]
                tau = stats_ref[h, 0:1, cs]
                s1 = st_ref[2 * h + 1, :, cs]
                e1 = e1_ref[h, :, cs]
                g = g + jnp.where(s0 + s1 >= tau, e0 * e1, 0.0)
            a = act_ref[rs, cs]
            ge = 0.5 * a * (1.0 + lax.erf(a * np.float32(math.sqrt(0.5))))
            w_ref[rs, cs] = (g * ge).astype(BF16)
        return carry

    lax.fori_loop(0, tt // LANES, lane_chunk, 0)
    out_ref[...] += jnp.dot(vt_ref[...], w_ref[...], preferred_element_type=F32)


def peer_pallas(h2, wq, keys, u_tab, v_tab, *, tt, eb, topk):
    T, D = h2.shape
    nh = keys.shape[0]
    nk = keys.shape[2]
    assert nk == LANES and keys.shape[3] == LANES and T % tt == 0
    E = nk * nk
    wq_b = wq.astype(BF16)
    keys_b = keys.reshape(2 * nh, nk, LANES).astype(BF16)
    nt = T // tt
    st, stats, hb = pl.pallas_call(
        functools.partial(_peer_route_kernel, nh=nh, topk=topk),
        grid=(nt,),
        in_specs=[pl.BlockSpec((tt, D), lambda t: (t, 0)),
                  pl.BlockSpec((D, 2 * nh * LANES), lambda t: (0, 0)),
                  pl.BlockSpec((2 * nh, nk, LANES), lambda t: (0, 0, 0))],
        out_specs=[pl.BlockSpec((2 * nh, nk, tt), lambda t: (0, 0, t)),
                   pl.BlockSpec((nh, 8, tt), lambda t: (0, 0, t)),
                   pl.BlockSpec((tt, D), lambda t: (t, 0))],
        out_shape=[jax.ShapeDtypeStruct((2 * nh, nk, T), F32),
                   jax.ShapeDtypeStruct((nh, 8, T), F32),
                   jax.ShapeDtypeStruct((T, D), BF16)],
        compiler_params=pltpu.CompilerParams(dimension_semantics=("parallel",),
                                             vmem_limit_bytes=VMEM_LIMIT),
        name="peer_route",
    )(h2, wq_b, keys_b)
    u_b = u_tab.astype(BF16)
    vt_b = v_tab.T.astype(BF16)
    out_t = pl.pallas_call(
        functools.partial(_peer_main_kernel, nh=nh, eb=eb, tt=tt),
        grid=(nt, E // eb),
        in_specs=[pl.BlockSpec((tt, D), lambda t, i: (t, 0)),
                  pl.BlockSpec((2 * nh, nk, tt), lambda t, i: (0, 0, t)),
                  pl.BlockSpec((nh, 8, tt), lambda t, i: (0, 0, t)),
                  pl.BlockSpec((eb, D), lambda t, i: (i, 0)),
                  pl.BlockSpec((D, eb), lambda t, i: (0, i))],
        out_specs=pl.BlockSpec((D, tt), lambda t, i: (0, t)),
        out_shape=jax.ShapeDtypeStruct((D, T), F32),
        scratch_shapes=[pltpu.VMEM((nh, nk, tt), F32),
                        pltpu.VMEM((nh, nk, tt), F32),
                        pltpu.VMEM((eb, tt), F32),
                        pltpu.VMEM((eb, tt), BF16)],
        compiler_params=pltpu.CompilerParams(dimension_semantics=("parallel", "arbitrary"),
                                             vmem_limit_bytes=VMEM_LIMIT),
        name="peer_main",
    )(hb, st, stats, u_b, vt_b)
    return out_t.T


def peer_tokens(hf, wq, sub_keys, u_tab, v_tab):
    return peer_pallas(hf, wq, sub_keys, u_tab, v_tab, tt=PEER_TT, eb=PEER_EB, topk=PEER_TOPK)


def _final_norm_kernel(x_ref, g_ref, o_ref):
    x = x_ref[...]
    y = x * lax.rsqrt(jnp.mean(x * x, axis=-1, keepdims=True) + EPS)
    o_ref[...] = y * g_ref[...]


def final_norm(x, g):
    T = x.shape[0]
    tm = 512
    return pl.pallas_call(
        _final_norm_kernel,
        grid=(T // tm,),
        in_specs=[pl.BlockSpec((tm, D_MODEL), lambda i: (i, 0)),
                  pl.BlockSpec((1, D_MODEL), lambda i: (0, 0))],
        out_specs=pl.BlockSpec((tm, D_MODEL), lambda i: (i, 0)),
        out_shape=jax.ShapeDtypeStruct((T, D_MODEL), F32),
        name="final_norm",
    )(x, g.reshape(1, D_MODEL))


def kernel(x, c, ctx, c_ctx, w_mod, b_mod, norm1_g, norm2_g, w_in, b_gate, conv_w, conv_b,
           conv_ln_g, conv_ln_b, w_conv_o, gla_wa2, gla_ba, gla_norm_g, w_gla_o, na_rpb, w_na_o,
           w_out, peer_wq, peer_keys, peer_u, peer_v, final_g):
    xc = ctx
    cond = jax.nn.silu(c.astype(F32))
    cond_ctx = jax.nn.silu(c_ctx.astype(F32))
    for i in range(DEPTH):
        last = i == DEPTH - 1
        mod = (cond @ w_mod[i].astype(F32) + b_mod[i].astype(F32)).astype(x.dtype)[:, None, :]
        modc = (cond_ctx @ w_mod[i].astype(F32) + b_mod[i].astype(F32)).astype(x.dtype)
        sh1, sc1, g1, sh2, sc2, g2 = jnp.split(mod, 6, axis=-1)
        csh1, csc1, cg1, csh2, csc2, cg2 = jnp.split(modc, 6, axis=-1)
        h = rmsnorm(x, norm1_g[i]) * (1 + sc1) + sh1
        hc = rmsnorm(xc, norm1_g[i]) * (1 + csc1) + csh1
        y, yc = token_mixer(h, hc, not last, w_in[i], b_gate[i], conv_w[i], conv_b[i], conv_ln_g[i],
                            conv_ln_b[i], w_conv_o[i], gla_wa2[i], gla_ba[i], gla_norm_g[i], w_gla_o[i],
                            na_rpb[i], w_na_o[i], w_out[i])
        x = x + g1 * y.astype(x.dtype)
        h2 = rmsnorm(x, norm2_g[i]) * (1 + sc2) + sh2
        toks = h2.reshape(BATCH * SEQ, D_MODEL)
        if not last:
            xc = xc + cg1 * yc.astype(xc.dtype)
            hc2 = rmsnorm(xc, norm2_g[i]) * (1 + csc2) + csh2
            toks = jnp.concatenate([toks, hc2.reshape(BATCH * CTX_LEN, D_MODEL)], axis=0)
        p = peer_tokens(toks, peer_wq[i], peer_keys[i], peer_u[i], peer_v[i])
        x = x + g2 * p[:BATCH * SEQ].reshape(BATCH, SEQ, D_MODEL)
        if not last:
            xc = xc + cg2 * p[BATCH * SEQ:].reshape(BATCH, CTX_LEN, D_MODEL)
    return final_norm(x.reshape(BATCH * SEQ, D_MODEL), final_g).reshape(BATCH, SEQ, D_MODEL)
```

```python
import math
import functools
import jax
import jax.numpy as jnp
from jax import lax
import numpy as np
from jax.experimental import pallas as pl
from jax.experimental.pallas import tpu as pltpu

D_MODEL = 1024
BATCH = 8
SEQ = 4096
DEPTH = 2
GRID_W = 64
CTX_LEN = 256
EPS = 1e-6
N_BRANCH = 3
F32 = jnp.float32
BF16 = jnp.bfloat16
CONV_DIM = 512
CONV_WIDTH = 31
GLA_HEADS = 4
GLA_DK = 64
GLA_DV = 128
GLA_RANK = 16
GLA_GATE_NORM = 16.0
GLA_CHUNK = 64
GLA_QK = GLA_HEADS * GLA_DK
GLA_V = GLA_HEADS * GLA_DV
ROPE_BASE = 10000.0
NA_HEADS = 8
NA_HD = 64
NA_DIM = NA_HEADS * NA_HD
NA_KH = 8
NA_KW = 16
NA_QR = 8
NA_QC = 16
PEER_HEADS = 8
PEER_NKEYS = 128
PEER_EXPERTS = PEER_NKEYS * PEER_NKEYS
PEER_TOPK = 16
PEER_DKEY = 256
PEER_TT = 1024
PEER_EB = 1024
IN_SPLITS = (2 * CONV_DIM, GLA_QK, GLA_QK, GLA_V, GLA_V, 2 * GLA_RANK, NA_DIM, NA_DIM, NA_DIM, N_BRANCH * D_MODEL)
IN_DIM = sum(IN_SPLITS)


def rmsnorm(x, g):
    xf = x.astype(F32)
    y = xf * lax.rsqrt(jnp.mean(xf * xf, axis=-1, keepdims=True) + EPS)
    return (y * g.astype(F32)).astype(x.dtype)


def layernorm(x, g, b):
    xf = x.astype(F32)
    mu = jnp.mean(xf, axis=-1, keepdims=True)
    xc = xf - mu
    var = jnp.mean(xc * xc, axis=-1, keepdims=True)
    return (xc * lax.rsqrt(var + EPS) * g.astype(F32) + b.astype(F32)).astype(x.dtype)


def split_cols(p):
    return jnp.split(p, np.cumsum(IN_SPLITS)[:-1].tolist(), axis=-1)


def conv_module(glu_in, conv_w, conv_b, ln_g, ln_b, w_o):
    u = glu_in[..., :CONV_DIM] * jax.nn.sigmoid(glu_in[..., CONV_DIM:])
    u = lax.conv_general_dilated(
        u, conv_w.astype(u.dtype)[:, None, :], (1,), [(CONV_WIDTH // 2, CONV_WIDTH // 2)],
        dimension_numbers=('NWC', 'WIO', 'NWC'), feature_group_count=CONV_DIM) + conv_b.astype(u.dtype)
    u = jax.nn.silu(layernorm(u, ln_g, ln_b))
    return u @ w_o


LANES = 128
SUBLANES = 8
NT_DIMS = (((1,), (1,)), ((), ()))
VMEM_LIMIT = 56 * 1024 * 1024
GLA_BLOCK = 256


def gla_rope_tables(n_ctx, n_lat):
    t = jnp.arange(n_lat)
    row = (t // GRID_W).astype(F32)
    col = (t % GRID_W).astype(F32)
    n_freq = GLA_DK // 4
    inv = ROPE_BASE ** (-jnp.arange(n_freq, dtype=F32) / n_freq)
    ang = jnp.concatenate([row[:, None] * inv, col[:, None] * inv], axis=-1)
    cos = jnp.repeat(jnp.cos(ang), 2, axis=-1)
    sin = jnp.repeat(jnp.sin(ang), 2, axis=-1)
    sign = jnp.tile(jnp.array([-1.0, 1.0], F32), GLA_DK // 2)
    cos = jnp.tile(cos, (1, GLA_HEADS))
    sin = jnp.tile(sin * sign, (1, GLA_HEADS))
    cos = jnp.concatenate([jnp.ones((n_ctx, GLA_QK), F32), cos], axis=0)
    sin = jnp.concatenate([jnp.zeros((n_ctx, GLA_QK), F32), sin], axis=0)
    return cos, sin


def _swap_pairs(x):
    lane = lax.broadcasted_iota(jnp.int32, x.shape, 1)
    return jnp.where(lane % 2 == 0, pltpu.roll(x, LANES - 1, axis=1), pltpu.roll(x, 1, axis=1))


def _rope(x, cos, sin):
    return jnp.concatenate([x[:, p * LANES:(p + 1) * LANES] * cos[:, p * LANES:(p + 1) * LANES]
                            + _swap_pairs(x[:, p * LANES:(p + 1) * LANES]) * sin[:, p * LANES:(p + 1) * LANES]
                            for p in range(GLA_QK // LANES)], axis=1)


def _split3(x):
    hi = x.astype(BF16)
    r = x - hi.astype(F32)
    mid = r.astype(BF16)
    lo = (r - mid.astype(F32)).astype(BF16)
    return hi, mid, lo


def _gla_chunk(q, k, v, a, cos, sin, wa, ba, state_ref, backward):
    C = GLA_CHUNK
    qr = _rope(q * np.float32(GLA_DK ** -0.5), cos, sin)
    kr = _rope(k, cos, sin)
    z = jnp.dot(a.astype(BF16), wa, preferred_element_type=F32) + ba
    la = (jnp.minimum(z, 0.0) - jnp.log(1.0 + jnp.exp(-jnp.abs(z)))) * np.float32(1.0 / GLA_GATE_NORM)
    ri = lax.broadcasted_iota(jnp.int32, (C, C), 0)
    ci = lax.broadcasted_iota(jnp.int32, (C, C), 1)
    keep = (ci >= ri) if backward else (ci <= ri)
    tri = jnp.where(keep, 1.0, 0.0).astype(BF16)
    hi, mid, lo = _split3(la)
    b = (jnp.dot(tri, hi, preferred_element_type=F32) + jnp.dot(tri, mid, preferred_element_type=F32)
         + jnp.dot(tri, lo, preferred_element_type=F32))
    mid_row = C // 2 - 1 if backward else C // 2
    end_row = 0 if backward else C - 1
    b_mid = b[mid_row:mid_row + 1]
    b_end = b[end_row:end_row + 1]
    qe = qr * jnp.exp(b)
    qa = qr * jnp.exp(b - b_mid)
    ka = kr * jnp.exp(b_mid - b)
    ke = kr * jnp.exp(b_end - b)
    decay = jnp.exp(b_end)
    lane = lax.broadcasted_iota(jnp.int32, (C, LANES), 1)
    row128 = lax.broadcasted_iota(jnp.int32, (LANES, LANES), 0)
    outs = []
    for p in range(GLA_QK // LANES):
        ps = slice(p * LANES, (p + 1) * LANES)
        s_prev = state_ref[p]
        s_b = s_prev.astype(BF16)
        ka_p = ka[:, ps].astype(BF16)
        ke_t = ke[:, ps].T.astype(BF16)
        upd = []
        for hh in range(2):
            h = 2 * p + hh
            in_head = (lane >= hh * GLA_DK) & (lane < (hh + 1) * GLA_DK)
            v_h = v[:, h * GLA_DV:(h + 1) * GLA_DV].astype(BF16)
            att = lax.dot_general(jnp.where(in_head, qa[:, ps], 0.0).astype(BF16), ka_p, NT_DIMS,
                                  preferred_element_type=F32)
            att = jnp.where(keep, att, 0.0)
            o_h = (jnp.dot(att.astype(BF16), v_h, preferred_element_type=F32)
                   + jnp.dot(jnp.where(in_head, qe[:, ps], 0.0).astype(BF16), s_b, preferred_element_type=F32))
            outs.append(o_h)
            upd.append(jnp.dot(ke_t, v_h, preferred_element_type=F32))
        d_col = jnp.broadcast_to(decay[:, ps], (LANES, LANES)).T
        state_ref[p] = s_prev * d_col + jnp.where(row128 < GLA_DK, upd[0], upd[1])
    return jnp.concatenate(outs, axis=1)


def _gla_kernel(qf_ref, kf_ref, vf_ref, af_ref, cosf_ref, sinf_ref,
                qb_ref, kb_ref, vb_ref, ab_ref, cosb_ref, sinb_ref,
                wa_ref, ba_ref, of_ref, ob_ref, sf_ref, sb_ref):
    n = pl.program_id(1)

    @pl.when(n == 0)
    def _():
        sf_ref[...] = jnp.zeros(sf_ref.shape, F32)
        sb_ref[...] = jnp.zeros(sb_ref.shape, F32)

    nch = GLA_BLOCK // GLA_CHUNK
    for j in range(nch):
        rf = slice(j * GLA_CHUNK, (j + 1) * GLA_CHUNK)
        of_ref[0, rf, :] = _gla_chunk(qf_ref[0, rf, :], kf_ref[0, rf, :], vf_ref[0, rf, :], af_ref[0, rf, :],
                                      cosf_ref[rf, :], sinf_ref[rf, :], wa_ref[0], ba_ref[0:1, :], sf_ref, False)
        jb = nch - 1 - j
        rb = slice(jb * GLA_CHUNK, (jb + 1) * GLA_CHUNK)
        ob_ref[0, rb, :] = _gla_chunk(qb_ref[0, rb, :], kb_ref[0, rb, :], vb_ref[0, rb, :], ab_ref[0, rb, :],
                                      cosb_ref[rb, :], sinb_ref[rb, :], wa_ref[1], ba_ref[1:2, :], sb_ref, True)


def gla_bidir(q, k, v, a, wa2, ba, n_ctx):
    B, L, _ = q.shape
    nb = L // GLA_BLOCK
    assert n_ctx == GLA_BLOCK and L % GLA_BLOCK == 0
    cos, sin = gla_rope_tables(n_ctx, L - n_ctx)
    zero = jnp.zeros((GLA_RANK, GLA_QK), F32)
    wa = jnp.stack([jnp.concatenate([wa2[0], zero], axis=0), jnp.concatenate([zero, wa2[1]], axis=0)]).astype(BF16)

    def fwd(b, n):
        return (b, n, 0)

    def bwd(b, n):
        return (b, jnp.where(n == 0, 0, nb - n), 0)

    def rows_only(f):
        return lambda b, n: f(b, n)[1:]

    def specs(f):
        return [pl.BlockSpec((1, GLA_BLOCK, GLA_QK), f), pl.BlockSpec((1, GLA_BLOCK, GLA_QK), f),
                pl.BlockSpec((1, GLA_BLOCK, GLA_V), f), pl.BlockSpec((1, GLA_BLOCK, 2 * GLA_RANK), f),
                pl.BlockSpec((GLA_BLOCK, GLA_QK), rows_only(f)), pl.BlockSpec((GLA_BLOCK, GLA_QK), rows_only(f))]

    return pl.pallas_call(
        _gla_kernel,
        grid=(B, nb),
        in_specs=specs(fwd) + specs(bwd) + [pl.BlockSpec((2, 2 * GLA_RANK, GLA_QK), lambda b, n: (0, 0, 0)),
                                            pl.BlockSpec((2, GLA_QK), lambda b, n: (0, 0))],
        out_specs=[pl.BlockSpec((1, GLA_BLOCK, GLA_V), fwd), pl.BlockSpec((1, GLA_BLOCK, GLA_V), bwd)],
        out_shape=[jax.ShapeDtypeStruct((B, L, GLA_V), F32), jax.ShapeDtypeStruct((B, L, GLA_V), F32)],
        scratch_shapes=[pltpu.VMEM((GLA_QK // LANES, LANES, GLA_DV), F32),
                        pltpu.VMEM((GLA_QK // LANES, LANES, GLA_DV), F32)],
        compiler_params=pltpu.CompilerParams(dimension_semantics=("parallel", "arbitrary"),
                                             vmem_limit_bytes=VMEM_LIMIT),
        name="gla_scan",
    )(q, k, v, a, cos, sin, q, k, v, a, cos, sin, wa, ba)


def gla_output(o, pg, norm_g, w_o):
    B, L = o.shape[:2]
    on = rmsnorm(o, norm_g.reshape(GLA_HEADS, GLA_DV)).reshape(B, L, GLA_V)
    return (on.astype(pg.dtype) * jax.nn.silu(pg)) @ w_o


NA_QROWS = 8
NA_KROWS = 16
NA_QCHUNK = 256


def na_key_row_start(x, rows):
    return min(max(NA_QROWS * x - NA_KH // 2, 0), rows - NA_KROWS)


def na_bias_blocks(rpb, rows):
    W = GRID_W
    rs = np.clip(np.arange(rows) - NA_KH // 2, 0, rows - NA_KH)
    cs = np.clip(np.arange(W) - NA_KW // 2, 0, W - NA_KW)
    nrb = rows // NA_QROWS
    blocks = (0, 1, nrb - 1)
    qr = np.stack([np.arange(NA_QROWS) + NA_QROWS * x for x in blocks])
    kr = np.stack([na_key_row_start(x, rows) + np.arange(NA_KROWS) for x in blocks])
    ok_r = (kr[:, None, :] >= rs[qr][:, :, None]) & (kr[:, None, :] < rs[qr][:, :, None] + NA_KH)
    i_r = np.clip(kr[:, None, :] - qr[:, :, None] + NA_KH - 1, 0, 2 * NA_KH - 2)
    qc = np.arange(W)
    kc = np.arange(W)
    ok_c = (kc[None, :] >= cs[qc][:, None]) & (kc[None, :] < cs[qc][:, None] + NA_KW)
    i_c = np.clip(kc[None, :] - qc[:, None] + NA_KW - 1, 0, 2 * NA_KW - 2)
    ok = ok_r[:, :, None, :, None] & ok_c[None, None, :, None, :]
    pick_r = (i_r[..., None] == np.arange(2 * NA_KH - 1)).astype(np.float32)
    pick_c = (i_c[..., None] == np.arange(2 * NA_KW - 1)).astype(np.float32)
    rows_sel = jnp.einsum('tajr,hrc->htajc', pick_r, rpb.astype(F32), precision=lax.Precision.HIGHEST)
    vals = jnp.einsum('htajc,qkc->htaqjk', rows_sel, pick_c, precision=lax.Precision.HIGHEST)
    bias = jnp.where(ok[None], vals, -jnp.inf)
    return bias.reshape(rpb.shape[0], 3, NA_QROWS * W, NA_KROWS * W)


def _na_kernel(q_ref, k_ref, v_ref, kc_ref, vc_ref, bias_ref, o_ref, *, rows):
    x = pl.program_id(2)
    W = GRID_W
    start_row = jnp.minimum(jnp.maximum(NA_QROWS * x - NA_KH // 2, 0), rows - NA_KROWS)
    win = pl.ds(pl.multiple_of(start_row * W, W), NA_KROWS * W)
    kw = k_ref[0, win, :].astype(BF16)
    vw = v_ref[0, win, :].astype(BF16)
    kc = kc_ref[0].astype(BF16)
    vc = vc_ref[0].astype(BF16)
    lane = lax.broadcasted_iota(jnp.int32, (NA_QCHUNK, LANES), 1)
    scale = np.float32(NA_HD ** -0.5)
    nq = NA_QROWS * W
    for c in range(nq // NA_QCHUNK):
        qs = slice(c * NA_QCHUNK, (c + 1) * NA_QCHUNK)
        q = q_ref[0, qs, :] * scale
        outs = []
        for hh in range(2):
            in_head = (lane >= hh * NA_HD) & (lane < (hh + 1) * NA_HD)
            qm = jnp.where(in_head, q, 0.0).astype(BF16)
            s = lax.dot_general(qm, kw, NT_DIMS, preferred_element_type=F32) + bias_ref[hh, 0, qs, :]
            sc = lax.dot_general(qm, kc, NT_DIMS, preferred_element_type=F32)
            m = jnp.maximum(jnp.max(s, axis=-1, keepdims=True), jnp.max(sc, axis=-1, keepdims=True))
            p = jnp.exp(s - m)
            pc = jnp.exp(sc - m)
            l = jnp.sum(p, axis=-1, keepdims=True) + jnp.sum(pc, axis=-1, keepdims=True)
            o = (jnp.dot(p.astype(BF16), vw, preferred_element_type=F32)
                 + jnp.dot(pc.astype(BF16), vc, preferred_element_type=F32))
            outs.append(o / l)
        o_ref[0, qs, :] = jnp.where(lane < NA_HD, outs[0], outs[1])


def na_latent(q, k, v, kc, vc, rpb):
    B, S, HD = q.shape
    Lc = kc.shape[1]
    rows = S // GRID_W
    nrb = rows // NA_QROWS
    npair = HD // LANES
    bias = na_bias_blocks(rpb, rows)
    nq = NA_QROWS * GRID_W
    nk = NA_KROWS * GRID_W

    def bias_idx(p, b, x):
        return (p, jnp.where(x == 0, 0, jnp.where(x == nrb - 1, 2, 1)), 0, 0)

    return pl.pallas_call(
        functools.partial(_na_kernel, rows=rows),
        grid=(npair, B, nrb),
        in_specs=[pl.BlockSpec((1, nq, LANES), lambda p, b, x: (b, x, p)),
                  pl.BlockSpec((1, S, LANES), lambda p, b, x: (b, 0, p)),
                  pl.BlockSpec((1, S, LANES), lambda p, b, x: (b, 0, p)),
                  pl.BlockSpec((1, Lc, LANES), lambda p, b, x: (b, 0, p)),
                  pl.BlockSpec((1, Lc, LANES), lambda p, b, x: (b, 0, p)),
                  pl.BlockSpec((2, 1, nq, nk), bias_idx)],
        out_specs=pl.BlockSpec((1, nq, LANES), lambda p, b, x: (b, x, p)),
        out_shape=jax.ShapeDtypeStruct((B, S, HD), F32),
        compiler_params=pltpu.CompilerParams(dimension_semantics=("parallel", "parallel", "arbitrary"),
                                             vmem_limit_bytes=VMEM_LIMIT),
        name="na_latent",
    )(q, k, v, kc, vc, bias)


def na_context(qc, kc, vc):
    s = jnp.einsum('bihd,bjhd->bhij', qc * NA_HD ** -0.5, kc).astype(F32)
    p = jax.nn.softmax(s, axis=-1).astype(vc.dtype)
    o = jnp.einsum('bhij,bjhd->bihd', p, vc)
    return o.reshape(qc.shape[0], qc.shape[1], NA_DIM)


def merge_branches(pre_gate, b_gate, y_conv, y_gla, y_na, w_out):
    g = jax.nn.sigmoid((pre_gate + b_gate).astype(F32)).reshape(*pre_gate.shape[:-1], N_BRANCH, D_MODEL)
    m = g[..., 0, :] * y_conv + g[..., 1, :] * y_gla + g[..., 2, :] * y_na
    return m.astype(pre_gate.dtype) @ w_out


def token_mixer(h, hc, need_ctx, w_in, b_gate, conv_w, conv_b, conv_ln_g, conv_ln_b, w_conv_o,
                gla_wa2, gla_ba, gla_norm_g, w_gla_o, na_rpb, w_na_o, w_out):
    B, S, _ = h.shape
    Lc = hc.shape[1]
    l_glu, l_q, l_k, l_v, l_g, l_a, l_nq, l_nk, l_nv, l_br = split_cols(h @ w_in)
    c_glu, c_q, c_k, c_v, c_g, c_a, c_nq, c_nk, c_nv, c_br = split_cols(hc @ w_in)
    cat = lambda c_part, l_part: jnp.concatenate([c_part, l_part], axis=1)
    o_f, o_b = gla_bidir(cat(c_q, l_q), cat(c_k, l_k), cat(c_v, l_v), cat(c_a, l_a), gla_wa2, gla_ba, Lc)
    o_all = o_f + o_b
    co_gla = o_all[:, :Lc].reshape(B, Lc, GLA_HEADS, GLA_DV)
    o_gla = o_all[:, Lc:].reshape(B, S, GLA_HEADS, GLA_DV)
    heads = lambda t, L: t.reshape(B, L, NA_HEADS, NA_HD)
    nkc, nvc = heads(c_nk, Lc), heads(c_nv, Lc)
    y_na = na_latent(l_nq, l_nk, l_nv, c_nk, c_nv, na_rpb)
    y = merge_branches(l_br, b_gate,
                       conv_module(l_glu, conv_w, conv_b, conv_ln_g, conv_ln_b, w_conv_o),
                       gla_output(o_gla, l_g, gla_norm_g, w_gla_o),
                       y_na @ w_na_o, w_out)
    yc = None
    if need_ctx:
        yc = merge_branches(c_br, b_gate,
                            conv_module(c_glu, conv_w, conv_b, conv_ln_g, conv_ln_b, w_conv_o),
                            gla_output(co_gla, c_g, gla_norm_g, w_gla_o),
                            na_context(heads(c_nq, Lc), nkc, nvc) @ w_na_o, w_out)
    return y, yc


def _peer_route_kernel(h_ref, wq_ref, keys_ref, st_ref, stats_ref, hb_ref, *, nh, topk):
    tt = h_ref.shape[0]
    hb = h_ref[...].astype(BF16)
    hb_ref[...] = hb
    qry = jnp.dot(hb, wq_ref[...], preferred_element_type=F32)
    neg = jnp.float32(-jnp.inf)
    for h in range(nh):
        tops = []
        for z in range(2):
            hz = 2 * h + z
            q = qry[:, hz * LANES:(hz + 1) * LANES].astype(BF16)
            s = lax.dot_general(keys_ref[hz], q, NT_DIMS, preferred_element_type=F32)
            st_ref[hz] = s
            cur = s
            vals = []
            for _ in range(topk):
                m = jnp.max(cur, axis=0, keepdims=True)
                vals.append(m)
                cur = jnp.where(cur == m, neg, cur)
            tops.append(vals)
        a, b = tops
        cands = [a[p] + b[q] for p in range(topk) for q in range(topk) if (p + 1) * (q + 1) <= topk]
        npad = (-len(cands)) % 8
        cand = jnp.concatenate(cands + [jnp.full((npad, tt), neg, F32)], axis=0)
        cur = cand
        tau = None
        for _ in range(topk):
            tau = jnp.max(cur, axis=0, keepdims=True)
            cur = jnp.where(cur == tau, neg, cur)
        top = a[0] + b[0]
        zsum = jnp.sum(jnp.where(cand >= tau, jnp.exp(cand - top), 0.0), axis=0, keepdims=True)
        stats_ref[h] = jnp.concatenate([tau, a[0], b[0], 1.0 / zsum, jnp.zeros((4, tt), F32)], axis=0)


def _peer_main_kernel(hb_ref, st_ref, stats_ref, u_ref, vt_ref, out_ref, e0_ref, e1_ref, act_ref, w_ref,
                      *, nh, eb, tt):
    i = pl.program_id(1)

    @pl.when(i == 0)
    def _():
        out_ref[...] = jnp.zeros(out_ref.shape, F32)
        for h in range(nh):
            a1 = stats_ref[h, 1:2, :]
            b1 = stats_ref[h, 2:3, :]
            rz = stats_ref[h, 3:4, :]
            e0_ref[h] = jnp.exp(st_ref[2 * h] - a1) * rz
            e1_ref[h] = jnp.exp(st_ref[2 * h + 1] - b1)

    act_ref[...] = lax.dot_general(u_ref[...], hb_ref[...], NT_DIMS, preferred_element_type=F32)
    nsub = eb // LANES
    assert nsub == SUBLANES
    first = pl.ds(pl.multiple_of(i * SUBLANES, SUBLANES), SUBLANES)

    def lane_chunk(c, carry):
        cs = pl.ds(pl.multiple_of(c * LANES, LANES), LANES)
        for s in range(nsub):
            rs = slice(s * LANES, (s + 1) * LANES)
            g = jnp.zeros((LANES, LANES), F32)
            for h in range(nh):
                s0 = st_ref[2 * h, first, cs][s:s + 1]
                e0 = e0_ref[h, first, cs][s:s + 1]
                tau = stats_ref[h, 0:1, cs]
                s1 = st_ref[2 * h + 1, :, cs]
                e1 = e1_ref[h, :, cs]
                g = g + jnp.where(s0 + s1 >= tau, e0 * e1, 0.0)
            a = act_ref[rs, cs]
            ge = 0.5 * a * (1.0 + lax.erf(a * np.float32(math.sqrt(0.5))))
            w_ref[rs, cs] = (g * ge).astype(BF16)
        return carry

    lax.fori_loop(0, tt // LANES, lane_chunk, 0)
    out_ref[...] += jnp.dot(vt_ref[...], w_ref[...], preferred_element_type=F32)


def peer_pallas(h2, wq, keys, u_tab, v_tab, *, tt, eb, topk):
    T, D = h2.shape
    nh = keys.shape[0]
    nk = keys.shape[2]
    assert nk == LANES and keys.shape[3] == LANES and T % tt == 0
    E = nk * nk
    wq_b = wq.astype(BF16)
    keys_b = keys.reshape(2 * nh, nk, LANES).astype(BF16)
    nt = T // tt
    st, stats, hb = pl.pallas_call(
        functools.partial(_peer_route_kernel, nh=nh, topk=topk),
        grid=(nt,),
        in_specs=[pl.BlockSpec((tt, D), lambda t: (t, 0)),
                  pl.BlockSpec((D, 2 * nh * LANES), lambda t: (0, 0)),
                  pl.BlockSpec((2 * nh, nk, LANES), lambda t: (0, 0, 0))],
        out_specs=[pl.BlockSpec((2 * nh, nk, tt), lambda t: (0, 0, t)),
                   pl.BlockSpec((nh, 8, tt), lambda t: (0, 0, t)),
                   pl.BlockSpec((tt, D), lambda t: (t, 0))],
        out_shape=[jax.ShapeDtypeStruct((2 * nh, nk, T), F32),
                   jax.ShapeDtypeStruct((nh, 8, T), F32),
                   jax.ShapeDtypeStruct((T, D), BF16)],
        compiler_params=pltpu.CompilerParams(dimension_semantics=("parallel",),
                                             vmem_limit_bytes=VMEM_LIMIT),
        name="peer_route",
    )(h2, wq_b, keys_b)
    u_b = u_tab.astype(BF16)
    vt_b = v_tab.T.astype(BF16)
    out_t = pl.pallas_call(
        functools.partial(_peer_main_kernel, nh=nh, eb=eb, tt=tt),
        grid=(nt, E // eb),
        in_specs=[pl.BlockSpec((tt, D), lambda t, i: (t, 0)),
                  pl.BlockSpec((2 * nh, nk, tt), lambda t, i: (0, 0, t)),
                  pl.BlockSpec((nh, 8, tt), lambda t, i: (0, 0, t)),
                  pl.BlockSpec((eb, D), lambda t, i: (i, 0)),
                  pl.BlockSpec((D, eb), lambda t, i: (0, i))],
        out_specs=pl.BlockSpec((D, tt), lambda t, i: (0, t)),
        out_shape=jax.ShapeDtypeStruct((D, T), F32),
        scratch_shapes=[pltpu.VMEM((nh, nk, tt), F32),
                        pltpu.VMEM((nh, nk, tt), F32),
                        pltpu.VMEM((eb, tt), F32),
                        pltpu.VMEM((eb, tt), BF16)],
        compiler_params=pltpu.CompilerParams(dimension_semantics=("parallel", "arbitrary"),
                                             vmem_limit_bytes=VMEM_LIMIT),
        name="peer_main",
    )(hb, st, stats, u_b, vt_b)
    return out_t.T


def peer_tokens(hf, wq, sub_keys, u_tab, v_tab):
    return peer_pallas(hf, wq, sub_keys, u_tab, v_tab, tt=PEER_TT, eb=PEER_EB, topk=PEER_TOPK)


def _final_norm_kernel(x_ref, g_ref, o_ref):
    x = x_ref[...]
    y = x * lax.rsqrt(jnp.mean(x * x, axis=-1, keepdims=True) + EPS)
    o_ref[...] = y * g_ref[...]


def final_norm(x, g):
    T = x.shape[0]
    tm = 512
    return pl.pallas_call(
        _final_norm_kernel,
        grid=(T // tm,),
        in_specs=[pl.BlockSpec((tm, D_MODEL), lambda i: (i, 0)),
                  pl.BlockSpec((1, D_MODEL), lambda i: (0, 0))],
        out_specs=pl.BlockSpec((tm, D_MODEL), lambda i: (i, 0)),
        out_shape=jax.ShapeDtypeStruct((T, D_MODEL), F32),
        name="final_norm",
    )(x, g.reshape(1, D_MODEL))


def kernel(x, c, ctx, c_ctx, w_mod, b_mod, norm1_g, norm2_g, w_in, b_gate, conv_w, conv_b,
           conv_ln_g, conv_ln_b, w_conv_o, gla_wa2, gla_ba, gla_norm_g, w_gla_o, na_rpb, w_na_o,
           w_out, peer_wq, peer_keys, peer_u, peer_v, final_g):
    xc = ctx
    cond = jax.nn.silu(c.astype(F32))
    cond_ctx = jax.nn.silu(c_ctx.astype(F32))
    for i in range(DEPTH):
        last = i == DEPTH - 1
        mod = (cond @ w_mod[i].astype(F32) + b_mod[i].astype(F32)).astype(x.dtype)[:, None, :]
        modc = (cond_ctx @ w_mod[i].astype(F32) + b_mod[i].astype(F32)).astype(x.dtype)
        sh1, sc1, g1, sh2, sc2, g2 = jnp.split(mod, 6, axis=-1)
        csh1, csc1, cg1, csh2, csc2, cg2 = jnp.split(modc, 6, axis=-1)
        h = rmsnorm(x, norm1_g[i]) * (1 + sc1) + sh1
        hc = rmsnorm(xc, norm1_g[i]) * (1 + csc1) + csh1
        y, yc = token_mixer(h, hc, not last, w_in[i], b_gate[i], conv_w[i], conv_b[i], conv_ln_g[i],
                            conv_ln_b[i], w_conv_o[i], gla_wa2[i], gla_ba[i], gla_norm_g[i], w_gla_o[i],
                            na_rpb[i], w_na_o[i], w_out[i])
        x = x + g1 * y.astype(x.dtype)
        h2 = rmsnorm(x, norm2_g[i]) * (1 + sc2) + sh2
        toks = h2.reshape(BATCH * SEQ, D_MODEL)
        if not last:
            xc = xc + cg1 * yc.astype(xc.dtype)
            hc2 = rmsnorm(xc, norm2_g[i]) * (1 + csc2) + csh2
            toks = jnp.concatenate([toks, hc2.reshape(BATCH * CTX_LEN, D_MODEL)], axis=0)
        p = peer_tokens(toks, peer_wq[i], peer_keys[i], peer_u[i], peer_v[i])
        x = x + g2 * p[:BATCH * SEQ].reshape(BATCH, SEQ, D_MODEL)
        if not last:
            xc = xc + cg2 * p[BATCH * SEQ:].reshape(BATCH, CTX_LEN, D_MODEL)
    return final_norm(x.reshape(BATCH * SEQ, D_MODEL), final_g).reshape(BATCH, SEQ, D_MODEL)
```

```python
import functools
import math
import jax
import jax.numpy as jnp
from jax import lax
import numpy as np
from jax.experimental import pallas as pl
from jax.experimental.pallas import tpu as pltpu

D_MODEL = 1024
BATCH = 8
SEQ = 4096
DEPTH = 2
GRID_W = 64
CTX_LEN = 256
EPS = 1e-6
N_BRANCH = 3
F32 = jnp.float32
BF16 = jnp.bfloat16
CONV_DIM = 512
CONV_WIDTH = 31
GLA_HEADS = 4
GLA_DK = 64
GLA_DV = 128
GLA_RANK = 16
GLA_GATE_NORM = 16.0
GLA_CHUNK = 64
GLA_QK = GLA_HEADS * GLA_DK
GLA_V = GLA_HEADS * GLA_DV
ROPE_BASE = 10000.0
NA_HEADS = 8
NA_HD = 64
NA_DIM = NA_HEADS * NA_HD
NA_KH = 8
NA_KW = 16
PEER_HEADS = 8
PEER_NKEYS = 128
PEER_TOPK = 16
PEER_DKEY = 256

LANES = 128
SUBLANES = 8
NT_DIMS = (((1,), (1,)), ((), ()))
VMEM_LIMIT = 56 * 1024 * 1024

SUB = CTX_LEN
ROW_TILE = 512
PEER_TT = 1024
PEER_EB = 1024
NA_QROWS = 4
NA_KROWS = 12

COL_BR = 0
COL_GLU = COL_BR + N_BRANCH * D_MODEL
COL_V = COL_GLU + 2 * CONV_DIM
COL_G = COL_V + GLA_V
COL_NQ = COL_G + GLA_V
COL_NK = COL_NQ + NA_DIM
COL_NV = COL_NK + NA_DIM
COL_Q = COL_NV + NA_DIM
COL_K = COL_Q + GLA_QK
COL_A = COL_K + GLA_QK
IN_COLS = COL_A + LANES
IN_COL_TILE = IN_COLS // 3
REF_SPLITS = (2 * CONV_DIM, GLA_QK, GLA_QK, GLA_V, GLA_V, 2 * GLA_RANK, NA_DIM, NA_DIM, NA_DIM, N_BRANCH * D_MODEL)


def _params(*sem):
    return pltpu.CompilerParams(dimension_semantics=sem, vmem_limit_bytes=VMEM_LIMIT)


def _mod_index(block, nb, nbatch):
    return jnp.where(block % nb == 0, nbatch, block // nb)


def _sigmoid(x):
    return 1.0 / (1.0 + jnp.exp(-x))


def _rms(x):
    return x * lax.rsqrt(jnp.mean(x * x, axis=-1, keepdims=True) + EPS)


def _mod_kernel(c_ref, w_ref, b_ref, o_ref):
    c = c_ref[...]
    cond = (c * _sigmoid(c)).astype(BF16)
    o_ref[...] = jnp.dot(cond, w_ref[...].astype(BF16), preferred_element_type=F32) + b_ref[...]


def mod_table(c, c_ctx, w_mod, b_mod):
    nb = c.shape[0]
    cond = jnp.concatenate([c, c_ctx[None], jnp.zeros((2 * SUBLANES - nb - 1, D_MODEL), F32)], axis=0)
    n = 6 * D_MODEL
    tn = n // 4
    out = pl.pallas_call(
        _mod_kernel,
        grid=(n // tn,),
        in_specs=[pl.BlockSpec((2 * SUBLANES, D_MODEL), lambda j: (0, 0)),
                  pl.BlockSpec((D_MODEL, tn), lambda j: (0, j)),
                  pl.BlockSpec((1, tn), lambda j: (0, j))],
        out_specs=pl.BlockSpec((2 * SUBLANES, tn), lambda j: (0, j)),
        out_shape=jax.ShapeDtypeStruct((2 * SUBLANES, n), F32),
        compiler_params=_params("parallel"),
        name="mod_table",
    )(cond, w_mod, b_mod.reshape(1, n))
    tab = out[:nb + 1].reshape(nb + 1, 6, D_MODEL)
    return jnp.concatenate([tab, jnp.zeros((nb + 1, 2, D_MODEL), F32)], axis=1)


def _inproj_kernel(*refs, has_resid, nb, nbatch):
    if has_resid:
        x_ref, p_ref, modp_ref, mod_ref, g_ref, w_ref, o_ref, xo_ref, h_ref = refs
    else:
        x_ref, mod_ref, g_ref, w_ref, o_ref, h_ref = refs
    r = pl.program_id(0)

    @pl.when(pl.program_id(1) == 0)
    def _():
        for s in range(ROW_TILE // SUB):
            rows = slice(s * SUB, (s + 1) * SUB)
            idx = _mod_index(r * (ROW_TILE // SUB) + s, nb, nbatch)
            x = x_ref[rows, :]
            if has_resid:
                x = x + modp_ref[idx][5:6, :] * p_ref[rows, :]
                xo_ref[rows, :] = x
            m = mod_ref[idx]
            h_ref[rows, :] = (_rms(x) * g_ref[...] * (1.0 + m[1:2, :]) + m[0:1, :]).astype(BF16)

    o_ref[...] = jnp.dot(h_ref[...], w_ref[...], preferred_element_type=F32)


def in_projection(x, p, mod_prev, mod, norm_g, w_in_cols, nb, nbatch):
    R = x.shape[0]
    has_resid = p is not None
    row = pl.BlockSpec((ROW_TILE, D_MODEL), lambda r, j: (r, 0))
    tab = pl.BlockSpec((nbatch + 1, SUBLANES, D_MODEL), lambda r, j: (0, 0, 0))
    in_specs = [row] + ([row, tab] if has_resid else []) + [
        tab, pl.BlockSpec((1, D_MODEL), lambda r, j: (0, 0)),
        pl.BlockSpec((D_MODEL, IN_COL_TILE), lambda r, j: (0, j))]
    out_specs = [pl.BlockSpec((ROW_TILE, IN_COL_TILE), lambda r, j: (r, j))] + ([row] if has_resid else [])
    out_shape = [jax.ShapeDtypeStruct((R, IN_COLS), F32)] + (
        [jax.ShapeDtypeStruct((R, D_MODEL), F32)] if has_resid else [])
    args = [x] + ([p, mod_prev] if has_resid else []) + [mod, norm_g.reshape(1, D_MODEL), w_in_cols]
    outs = pl.pallas_call(
        functools.partial(_inproj_kernel, has_resid=has_resid, nb=nb, nbatch=nbatch),
        grid=(R // ROW_TILE, IN_COLS // IN_COL_TILE),
        in_specs=in_specs, out_specs=out_specs, out_shape=out_shape,
        scratch_shapes=[pltpu.VMEM((ROW_TILE, D_MODEL), BF16)],
        compiler_params=_params("parallel", "arbitrary"),
        name="in_projection",
    )(*args)
    return (outs[0], outs[1]) if has_resid else (outs[0], x)


def reorder_w_in(w_in):
    parts = dict(zip(("glu", "q", "k", "v", "g", "a", "nq", "nk", "nv", "br"),
                     jnp.split(w_in, np.cumsum(REF_SPLITS)[:-1].tolist(), axis=-1)))
    a_pad = jnp.zeros((D_MODEL, LANES - 2 * GLA_RANK), w_in.dtype)
    cols = [parts[n] for n in ("br", "glu", "v", "g", "nq", "nk", "nv", "q", "k", "a")] + [a_pad]
    return jnp.concatenate(cols, axis=-1).astype(BF16)


def _conv_kernel(prev_ref, cur_ref, next_ref, w_ref, b_ref, lg_ref, lb_ref, o_ref, ext_ref, *, nb):
    n = pl.program_id(1)
    half = CONV_WIDTH // 2
    pad = 2 * SUBLANES

    def glu(x):
        return x[:, :CONV_DIM] * _sigmoid(x[:, CONV_DIM:])

    has_prev = n >= 2
    has_next = (n >= 1) & (n <= nb - 2)
    ext_ref[0:pad, :] = jnp.where(has_prev, glu(prev_ref[0, SUB - pad:SUB, :]), 0.0)
    ext_ref[pad:pad + SUB, :] = glu(cur_ref[0])
    ext_ref[pad + SUB:pad + SUB + pad, :] = jnp.where(has_next, glu(next_ref[0, 0:pad, :]), 0.0)
    acc = jnp.zeros((SUB, CONV_DIM), F32)
    for j in range(CONV_WIDTH):
        off = pad - half + j
        acc = acc + ext_ref[off:off + SUB, :] * w_ref[j:j + 1, :]
    u = acc + b_ref[...]
    mu = jnp.mean(u, axis=-1, keepdims=True)
    uc = u - mu
    var = jnp.mean(uc * uc, axis=-1, keepdims=True)
    y = uc * lax.rsqrt(var + EPS) * lg_ref[...] + lb_ref[...]
    o_ref[0] = (y * _sigmoid(y)).astype(BF16)


def conv_branch(proj, conv_w, conv_b, ln_g, ln_b, nb):
    B, L, _ = proj.shape
    cb = COL_GLU // (2 * CONV_DIM)
    blk = (1, SUB, 2 * CONV_DIM)
    vec = pl.BlockSpec((1, CONV_DIM), lambda b, n: (0, 0))
    return pl.pallas_call(
        functools.partial(_conv_kernel, nb=nb),
        grid=(B, nb),
        in_specs=[pl.BlockSpec(blk, lambda b, n: (b, jnp.maximum(n - 1, 0), cb)),
                  pl.BlockSpec(blk, lambda b, n: (b, n, cb)),
                  pl.BlockSpec(blk, lambda b, n: (b, jnp.minimum(n + 1, nb - 1), cb)),
                  pl.BlockSpec((CONV_WIDTH + 1, CONV_DIM), lambda b, n: (0, 0)), vec, vec, vec],
        out_specs=pl.BlockSpec((1, SUB, CONV_DIM), lambda b, n: (b, n, 0)),
        out_shape=jax.ShapeDtypeStruct((B, L, CONV_DIM), BF16),
        scratch_shapes=[pltpu.VMEM((SUB + 4 * SUBLANES, CONV_DIM), F32)],
        compiler_params=_params("parallel", "parallel"),
        name="conv_branch",
    )(proj, proj, proj, jnp.concatenate([conv_w, jnp.zeros((1, CONV_DIM), F32)], axis=0),
      conv_b.reshape(1, CONV_DIM), ln_g.reshape(1, CONV_DIM), ln_b.reshape(1, CONV_DIM))


def gla_rope_tables(n_ctx, n_lat):
    t = jnp.arange(n_lat)
    row = (t // GRID_W).astype(F32)
    col = (t % GRID_W).astype(F32)
    n_freq = GLA_DK // 4
    inv = ROPE_BASE ** (-jnp.arange(n_freq, dtype=F32) / n_freq)
    ang = jnp.concatenate([row[:, None] * inv, col[:, None] * inv], axis=-1)
    cos = jnp.repeat(jnp.cos(ang), 2, axis=-1)
    sin = jnp.repeat(jnp.sin(ang), 2, axis=-1)
    sign = jnp.tile(jnp.array([-1.0, 1.0], F32), GLA_DK // 2)
    cos = jnp.tile(cos, (1, GLA_HEADS))
    sin = jnp.tile(sin * sign, (1, GLA_HEADS))
    cos = jnp.concatenate([jnp.ones((n_ctx, GLA_QK), F32), cos], axis=0)
    sin = jnp.concatenate([jnp.zeros((n_ctx, GLA_QK), F32), sin], axis=0)
    return cos, sin


def _swap_pairs(x):
    lane = lax.broadcasted_iota(jnp.int32, x.shape, 1)
    return jnp.where(lane % 2 == 0, pltpu.roll(x, LANES - 1, axis=1), pltpu.roll(x, 1, axis=1))


def _rope(x, cos, sin):
    return jnp.concatenate([x[:, p * LANES:(p + 1) * LANES] * cos[:, p * LANES:(p + 1) * LANES]
                            + _swap_pairs(x[:, p * LANES:(p + 1) * LANES]) * sin[:, p * LANES:(p + 1) * LANES]
                            for p in range(GLA_QK // LANES)], axis=1)


def _split3(x):
    hi = x.astype(BF16)
    r = x - hi.astype(F32)
    mid = r.astype(BF16)
    lo = (r - mid.astype(F32)).astype(BF16)
    return hi, mid, lo


def _gla_chunk(q, k, v, a, cos, sin, wa, ba, state_ref, backward):
    C = GLA_CHUNK
    qr = _rope(q * np.float32(GLA_DK ** -0.5), cos, sin)
    kr = _rope(k, cos, sin)
    z = jnp.dot(a.astype(BF16), wa, preferred_element_type=F32) + ba
    la = (jnp.minimum(z, 0.0) - jnp.log(1.0 + jnp.exp(-jnp.abs(z)))) * np.float32(1.0 / GLA_GATE_NORM)
    ri = lax.broadcasted_iota(jnp.int32, (C, C), 0)
    ci = lax.broadcasted_iota(jnp.int32, (C, C), 1)
    keep = (ci >= ri) if backward else (ci <= ri)
    tri = jnp.where(keep, 1.0, 0.0).astype(BF16)
    hi, mid, lo = _split3(la)
    b = (jnp.dot(tri, hi, preferred_element_type=F32) + jnp.dot(tri, mid, preferred_element_type=F32)
         + jnp.dot(tri, lo, preferred_element_type=F32))
    mid_row = C // 2 - 1 if backward else C // 2
    end_row = 0 if backward else C - 1
    b_mid = b[mid_row:mid_row + 1]
    b_end = b[end_row:end_row + 1]
    qe = qr * jnp.exp(b)
    qa = qr * jnp.exp(b - b_mid)
    ka = kr * jnp.exp(b_mid - b)
    ke = kr * jnp.exp(b_end - b)
    decay = jnp.exp(b_end)
    lane = lax.broadcasted_iota(jnp.int32, (C, LANES), 1)
    row128 = lax.broadcasted_iota(jnp.int32, (LANES, LANES), 0)
    outs = []
    for p in range(GLA_QK // LANES):
        ps = slice(p * LANES, (p + 1) * LANES)
        s_prev = state_ref[p]
        s_b = s_prev.astype(BF16)
        ka_p = ka[:, ps].astype(BF16)
        ke_t = ke[:, ps].T.astype(BF16)
        upd = []
        for hh in range(2):
            h = 2 * p + hh
            in_head = (lane >= hh * GLA_DK) & (lane < (hh + 1) * GLA_DK)
            v_h = v[:, h * GLA_DV:(h + 1) * GLA_DV].astype(BF16)
            att = lax.dot_general(jnp.where(in_head, qa[:, ps], 0.0).astype(BF16), ka_p, NT_DIMS,
                                  preferred_element_type=F32)
            att = jnp.where(keep, att, 0.0)
            o_h = (jnp.dot(att.astype(BF16), v_h, preferred_element_type=F32)
                   + jnp.dot(jnp.where(in_head, qe[:, ps], 0.0).astype(BF16), s_b, preferred_element_type=F32))
            outs.append(o_h)
            upd.append(jnp.dot(ke_t, v_h, preferred_element_type=F32))
        d_col = jnp.broadcast_to(decay[:, ps], (LANES, LANES)).T
        state_ref[p] = s_prev * d_col + jnp.where(row128 < GLA_DK, upd[0], upd[1])
    return jnp.concatenate(outs, axis=1)


def _gla_kernel(qf_ref, kf_ref, vf_ref, af_ref, cosf_ref, sinf_ref,
                qb_ref, kb_ref, vb_ref, ab_ref, cosb_ref, sinb_ref,
                wa_ref, ba_ref, of_ref, ob_ref, sf_ref, sb_ref):
    n = pl.program_id(1)

    @pl.when(n == 0)
    def _():
        sf_ref[...] = jnp.zeros(sf_ref.shape, F32)
        sb_ref[...] = jnp.zeros(sb_ref.shape, F32)

    nch = SUB // GLA_CHUNK
    for j in range(nch):
        rf = slice(j * GLA_CHUNK, (j + 1) * GLA_CHUNK)
        of_ref[0, rf, :] = _gla_chunk(qf_ref[0, rf, :], kf_ref[0, rf, :], vf_ref[0, rf, :], af_ref[0, rf, :],
                                      cosf_ref[rf, :], sinf_ref[rf, :], wa_ref[0], ba_ref[0:1, :], sf_ref, False)
        jb = nch - 1 - j
        rb = slice(jb * GLA_CHUNK, (jb + 1) * GLA_CHUNK)
        ob_ref[0, rb, :] = _gla_chunk(qb_ref[0, rb, :], kb_ref[0, rb, :], vb_ref[0, rb, :], ab_ref[0, rb, :],
                                      cosb_ref[rb, :], sinb_ref[rb, :], wa_ref[1], ba_ref[1:2, :], sb_ref, True)


def gla_branch(proj, wa2, ba, nb):
    B, L, _ = proj.shape
    cos, sin = gla_rope_tables(SUB, L - SUB)
    wa = jnp.zeros((2, LANES, GLA_QK), F32)
    wa = wa.at[0, :GLA_RANK].set(wa2[0]).at[1, GLA_RANK:2 * GLA_RANK].set(wa2[1]).astype(BF16)

    def fwd(b, n):
        return (b, n)

    def bwd(b, n):
        return (b, jnp.where(n == 0, 0, nb - n))

    def specs(f):
        def col(width, offset):
            return pl.BlockSpec((1, SUB, width), lambda b, n: f(b, n) + (offset // width,))
        tab = pl.BlockSpec((SUB, GLA_QK), lambda b, n: (f(b, n)[1], 0))
        return [col(GLA_QK, COL_Q), col(GLA_QK, COL_K), col(GLA_V, COL_V), col(LANES, COL_A), tab, tab]

    return pl.pallas_call(
        _gla_kernel,
        grid=(B, nb),
        in_specs=specs(fwd) + specs(bwd) + [pl.BlockSpec((2, LANES, GLA_QK), lambda b, n: (0, 0, 0)),
                                            pl.BlockSpec((2, GLA_QK), lambda b, n: (0, 0))],
        out_specs=[pl.BlockSpec((1, SUB, GLA_V), lambda b, n: fwd(b, n) + (0,)),
                   pl.BlockSpec((1, SUB, GLA_V), lambda b, n: bwd(b, n) + (0,))],
        out_shape=[jax.ShapeDtypeStruct((B, L, GLA_V), F32), jax.ShapeDtypeStruct((B, L, GLA_V), F32)],
        scratch_shapes=[pltpu.VMEM((GLA_QK // LANES, LANES, GLA_DV), F32),
                        pltpu.VMEM((GLA_QK // LANES, LANES, GLA_DV), F32)],
        compiler_params=_params("parallel", "arbitrary"),
        name="gla_scan",
    )(proj, proj, proj, proj, cos, sin, proj, proj, proj, proj, cos, sin, wa, ba)


def na_key_row_start(x, rows):
    return min(max(NA_QROWS * x - NA_KH // 2, 0), rows - NA_KROWS)


def na_bias_blocks(rpb, rows):
    W = GRID_W
    rs = np.clip(np.arange(rows) - NA_KH // 2, 0, rows - NA_KH)
    cs = np.clip(np.arange(W) - NA_KW // 2, 0, W - NA_KW)
    nrb = rows // NA_QROWS
    blocks = (0, 1, nrb - 1)
    qr = np.stack([np.arange(NA_QROWS) + NA_QROWS * x for x in blocks])
    kr = np.stack([na_key_row_start(x, rows) + np.arange(NA_KROWS) for x in blocks])
    ok_r = (kr[:, None, :] >= rs[qr][:, :, None]) & (kr[:, None, :] < rs[qr][:, :, None] + NA_KH)
    i_r = np.clip(kr[:, None, :] - qr[:, :, None] + NA_KH - 1, 0, 2 * NA_KH - 2)
    qc = np.arange(W)
    kc = np.arange(W)
    ok_c = (kc[None, :] >= cs[qc][:, None]) & (kc[None, :] < cs[qc][:, None] + NA_KW)
    i_c = np.clip(kc[None, :] - qc[:, None] + NA_KW - 1, 0, 2 * NA_KW - 2)
    ok = ok_r[:, :, None, :, None] & ok_c[None, None, :, None, :]
    pick_r = (i_r[..., None] == np.arange(2 * NA_KH - 1)).astype(np.float32)
    pick_c = (i_c[..., None] == np.arange(2 * NA_KW - 1)).astype(np.float32)
    rows_sel = jnp.einsum('tajr,hrc->htajc', pick_r, rpb.astype(F32), precision=lax.Precision.HIGHEST)
    vals = jnp.einsum('htajc,qkc->htaqjk', rows_sel, pick_c, precision=lax.Precision.HIGHEST)
    bias = jnp.where(ok[None], vals, -jnp.inf)
    return bias.reshape(rpb.shape[0], 3, NA_QROWS * W, NA_KROWS * W)


def _na_kernel(q_ref, k_ref, v_ref, bias_ref, o_ref, *, rows):
    n = pl.program_id(2)
    W = GRID_W
    q = q_ref[0] * np.float32(NA_HD ** -0.5)
    kc = k_ref[0, 0:SUB, :].astype(BF16)
    vc = v_ref[0, 0:SUB, :].astype(BF16)
    lane = lax.broadcasted_iota(jnp.int32, (SUB, LANES), 1)

    def head_q(hh):
        in_head = (lane >= hh * NA_HD) & (lane < (hh + 1) * NA_HD)
        return jnp.where(in_head, q, 0.0).astype(BF16)

    @pl.when(n == 0)
    def _():
        outs = []
        for hh in range(2):
            sc = lax.dot_general(head_q(hh), kc, NT_DIMS, preferred_element_type=F32)
            pc = jnp.exp(sc - jnp.max(sc, axis=-1, keepdims=True))
            o = jnp.dot(pc.astype(BF16), vc, preferred_element_type=F32)
            outs.append(o / jnp.sum(pc, axis=-1, keepdims=True))
        o_ref[0] = jnp.where(lane < NA_HD, outs[0], outs[1]).astype(BF16)

    @pl.when(n > 0)
    def _():
        start_row = jnp.minimum(jnp.maximum(NA_QROWS * (n - 1) - NA_KH // 2, 0), rows - NA_KROWS)
        win = pl.ds(pl.multiple_of(SUB + start_row * W, W), NA_KROWS * W)
        kw = k_ref[0, win, :].astype(BF16)
        vw = v_ref[0, win, :].astype(BF16)
        outs = []
        for hh in range(2):
            qm = head_q(hh)
            s = lax.dot_general(qm, kw, NT_DIMS, preferred_element_type=F32) + bias_ref[hh, 0]
            sc = lax.dot_general(qm, kc, NT_DIMS, preferred_element_type=F32)
            m = jnp.maximum(jnp.max(s, axis=-1, keepdims=True), jnp.max(sc, axis=-1, keepdims=True))
            p = jnp.exp(s - m)
            pc = jnp.exp(sc - m)
            l = jnp.sum(p, axis=-1, keepdims=True) + jnp.sum(pc, axis=-1, keepdims=True)
            o = (jnp.dot(p.astype(BF16), vw, preferred_element_type=F32)
                 + jnp.dot(pc.astype(BF16), vc, preferred_element_type=F32))
            outs.append(o / l)
        o_ref[0] = jnp.where(lane < NA_HD, outs[0], outs[1]).astype(BF16)


def na_branch(proj, rpb, nb):
    B, L, _ = proj.shape
    rows = (L - SUB) // GRID_W
    nrb = nb - 1
    bias = na_bias_blocks(rpb, rows)

    def bias_idx(p, b, n):
        return (p, jnp.where(n <= 1, 0, jnp.where(n == nrb, 2, 1)), 0, 0)

    return pl.pallas_call(
        functools.partial(_na_kernel, rows=rows),
        grid=(NA_DIM // LANES, B, nb),
        in_specs=[pl.BlockSpec((1, SUB, LANES), lambda p, b, n: (b, n, COL_NQ // LANES + p)),
                  pl.BlockSpec((1, L, LANES), lambda p, b, n: (b, 0, COL_NK // LANES + p)),
                  pl.BlockSpec((1, L, LANES), lambda p, b, n: (b, 0, COL_NV // LANES + p)),
                  pl.BlockSpec((2, 1, NA_QROWS * GRID_W, NA_KROWS * GRID_W), bias_idx)],
        out_specs=pl.BlockSpec((1, SUB, LANES), lambda p, b, n: (b, n, p)),
        out_shape=jax.ShapeDtypeStruct((B, L, NA_DIM), BF16),
        compiler_params=_params("parallel", "parallel", "arbitrary"),
        name="na_attention",
    )(proj, proj, proj, bias)


def _merge_kernel(conv_ref, of_ref, ob_ref, pg_ref, na_ref, br_ref, x_ref, mod_ref, wc_ref, wg_ref, wn_ref,
                  wo_ref, bg_ref, gn_ref, n2_ref, xo_ref, h2_ref, *, nb, nbatch):
    r = pl.program_id(0)
    o = of_ref[...] + ob_ref[...]
    on = jnp.concatenate([_rms(o[:, h * GLA_DV:(h + 1) * GLA_DV]) for h in range(GLA_HEADS)], axis=1) * gn_ref[...]
    pg = pg_ref[...]
    y_gla = jnp.dot((on * (pg * _sigmoid(pg))).astype(BF16), wg_ref[...], preferred_element_type=F32)
    y_conv = jnp.dot(conv_ref[...], wc_ref[...], preferred_element_type=F32)
    y_na = jnp.dot(na_ref[...], wn_ref[...], preferred_element_type=F32)
    m = (_sigmoid(br_ref[:, 0:D_MODEL] + bg_ref[:, 0:D_MODEL]) * y_conv
         + _sigmoid(br_ref[:, D_MODEL:2 * D_MODEL] + bg_ref[:, D_MODEL:2 * D_MODEL]) * y_gla
         + _sigmoid(br_ref[:, 2 * D_MODEL:3 * D_MODEL] + bg_ref[:, 2 * D_MODEL:3 * D_MODEL]) * y_na)
    y = jnp.dot(m.astype(BF16), wo_ref[...], preferred_element_type=F32)
    for s in range(ROW_TILE // SUB):
        rows = slice(s * SUB, (s + 1) * SUB)
        md = mod_ref[_mod_index(r * (ROW_TILE // SUB) + s, nb, nbatch)]
        x = x_ref[rows, :] + md[2:3, :] * y[rows, :]
        xo_ref[rows, :] = x
        h2_ref[rows, :] = (_rms(x) * n2_ref[...] * (1.0 + md[4:5, :]) + md[3:4, :]).astype(BF16)


def merge_branches(conv_act, o_f, o_b, y_na, proj, x, mod, w_conv_o, w_gla_o, w_na_o, w_out, b_gate,
                   gla_norm_g, norm2_g, nb, nbatch):
    R = x.shape[0]

    def rows(width, block=0):
        return pl.BlockSpec((ROW_TILE, width), lambda r: (r, block))

    def whole(shape):
        return pl.BlockSpec(shape, lambda r: (0,) * len(shape))

    return pl.pallas_call(
        functools.partial(_merge_kernel, nb=nb, nbatch=nbatch),
        grid=(R // ROW_TILE,),
        in_specs=[rows(CONV_DIM), rows(GLA_V), rows(GLA_V), rows(GLA_V, COL_G // GLA_V), rows(NA_DIM),
                  rows(N_BRANCH * D_MODEL, COL_BR // (N_BRANCH * D_MODEL)), rows(D_MODEL),
                  whole((nbatch + 1, SUBLANES, D_MODEL)),
                  whole((CONV_DIM, D_MODEL)), whole((GLA_V, D_MODEL)), whole((NA_DIM, D_MODEL)),
                  whole((D_MODEL, D_MODEL)), whole((1, N_BRANCH * D_MODEL)), whole((1, GLA_V)),
                  whole((1, D_MODEL))],
        out_specs=[rows(D_MODEL), rows(D_MODEL)],
        out_shape=[jax.ShapeDtypeStruct((R, D_MODEL), F32), jax.ShapeDtypeStruct((R, D_MODEL), BF16)],
        compiler_params=_params("parallel"),
        name="merge_branches",
    )(conv_act, o_f, o_b, proj, y_na, proj, x, mod, w_conv_o.astype(BF16), w_gla_o.astype(BF16),
      w_na_o.astype(BF16), w_out.astype(BF16), b_gate.reshape(1, -1), gla_norm_g.reshape(1, -1),
      norm2_g.reshape(1, -1))


def _peer_route_kernel(h_ref, wq_ref, keys_ref, st_ref, stats_ref, *, nh, topk):
    tt = h_ref.shape[0]
    qry = jnp.dot(h_ref[...], wq_ref[...], preferred_element_type=F32)
    neg = jnp.float32(-jnp.inf)
    for h in range(nh):
        tops = []
        for z in range(2):
            hz = 2 * h + z
            q = qry[:, hz * LANES:(hz + 1) * LANES].astype(BF16)
            s = lax.dot_general(keys_ref[hz], q, NT_DIMS, preferred_element_type=F32)
            st_ref[hz] = s
            cur = s
            vals = []
            for _ in range(topk):
                m = jnp.max(cur, axis=0, keepdims=True)
                vals.append(m)
                cur = jnp.where(cur == m, neg, cur)
            tops.append(vals)
        a, b = tops
        cands = [a[p] + b[q] for p in range(topk) for q in range(topk) if (p + 1) * (q + 1) <= topk]
        npad = (-len(cands)) % SUBLANES
        cand = jnp.concatenate(cands + [jnp.full((npad, tt), neg, F32)], axis=0)
        cur = cand
        tau = None
        for _ in range(topk):
            tau = jnp.max(cur, axis=0, keepdims=True)
            cur = jnp.where(cur == tau, neg, cur)
        top = a[0] + b[0]
        zsum = jnp.sum(jnp.where(cand >= tau, jnp.exp(cand - top), 0.0), axis=0, keepdims=True)
        stats_ref[h] = jnp.concatenate([tau, a[0], b[0], 1.0 / zsum, jnp.zeros((4, tt), F32)], axis=0)


def _peer_main_kernel(hb_ref, st_ref, stats_ref, u_ref, vt_ref, out_ref, acc_ref, e0_ref, e1_ref, act_ref, w_ref,
                      *, nh, eb, tt):
    i = pl.program_id(1)

    @pl.when(i == 0)
    def _():
        acc_ref[...] = jnp.zeros(acc_ref.shape, F32)
        for h in range(nh):
            a1 = stats_ref[h, 1:2, :]
            b1 = stats_ref[h, 2:3, :]
            rz = stats_ref[h, 3:4, :]
            e0_ref[h] = jnp.exp(st_ref[2 * h] - a1) * rz
            e1_ref[h] = jnp.exp(st_ref[2 * h + 1] - b1)

    act_ref[...] = lax.dot_general(u_ref[...], hb_ref[...], NT_DIMS, preferred_element_type=F32)
    nsub = eb // LANES
    assert nsub == SUBLANES
    first = pl.ds(pl.multiple_of(i * SUBLANES, SUBLANES), SUBLANES)

    def lane_chunk(c, carry):
        cs = pl.ds(pl.multiple_of(c * LANES, LANES), LANES)
        for s in range(nsub):
            rs = slice(s * LANES, (s + 1) * LANES)
            g = jnp.zeros((LANES, LANES), F32)
            for h in range(nh):
                s0 = st_ref[2 * h, first, cs][s:s + 1]
                e0 = e0_ref[h, first, cs][s:s + 1]
                tau = stats_ref[h, 0:1, cs]
                s1 = st_ref[2 * h + 1, :, cs]
                e1 = e1_ref[h, :, cs]
                g = g + jnp.where(s0 + s1 >= tau, e0 * e1, 0.0)
            a = act_ref[rs, cs]
            ge = 0.5 * a * (1.0 + lax.erf(a * np.float32(math.sqrt(0.5))))
            w_ref[rs, cs] = (g * ge).astype(BF16)
        return carry

    lax.fori_loop(0, tt // LANES, lane_chunk, 0)
    acc_ref[...] += jnp.dot(vt_ref[...], w_ref[...], preferred_element_type=F32)

    @pl.when(i == pl.num_programs(1) - 1)
    def _():
        out_ref[...] = acc_ref[...].T


def peer_ffn(h2, wq, keys, u_tab, v_tab):
    T, D = h2.shape
    tt, eb, topk = PEER_TT, PEER_EB, PEER_TOPK
    nh = keys.shape[0]
    nk = keys.shape[2]
    assert nk == LANES and keys.shape[3] == LANES and T % tt == 0
    E = nk * nk
    wq_b = wq.astype(BF16)
    keys_b = keys.reshape(2 * nh, nk, LANES).astype(BF16)
    nt = T // tt
    st, stats = pl.pallas_call(
        functools.partial(_peer_route_kernel, nh=nh, topk=topk),
        grid=(nt,),
        in_specs=[pl.BlockSpec((tt, D), lambda t: (t, 0)),
                  pl.BlockSpec((D, 2 * nh * LANES), lambda t: (0, 0)),
                  pl.BlockSpec((2 * nh, nk, LANES), lambda t: (0, 0, 0))],
        out_specs=[pl.BlockSpec((2 * nh, nk, tt), lambda t: (0, 0, t)),
                   pl.BlockSpec((nh, SUBLANES, tt), lambda t: (0, 0, t))],
        out_shape=[jax.ShapeDtypeStruct((2 * nh, nk, T), F32),
                   jax.ShapeDtypeStruct((nh, SUBLANES, T), F32)],
        compiler_params=_params("parallel"),
        name="peer_route",
    )(h2, wq_b, keys_b)
    u_b = u_tab.astype(BF16)
    vt_b = v_tab.T.astype(BF16)
    return pl.pallas_call(
        functools.partial(_peer_main_kernel, nh=nh, eb=eb, tt=tt),
        grid=(nt, E // eb),
        in_specs=[pl.BlockSpec((tt, D), lambda t, i: (t, 0), pipeline_mode=pl.Buffered(1)),
                  pl.BlockSpec((2 * nh, nk, tt), lambda t, i: (0, 0, t), pipeline_mode=pl.Buffered(1)),
                  pl.BlockSpec((nh, SUBLANES, tt), lambda t, i: (0, 0, t)),
                  pl.BlockSpec((eb, D), lambda t, i: (i, 0)),
                  pl.BlockSpec((D, eb), lambda t, i: (0, i))],
        out_specs=pl.BlockSpec((tt, D), lambda t, i: (t, 0)),
        out_shape=jax.ShapeDtypeStruct((T, D), F32),
        scratch_shapes=[pltpu.VMEM((D, tt), F32),
                        pltpu.VMEM((nh, nk, tt), F32),
                        pltpu.VMEM((nh, nk, tt), F32),
                        pltpu.VMEM((eb, tt), F32),
                        pltpu.VMEM((eb, tt), BF16)],
        compiler_params=_params("parallel", "arbitrary"),
        name="peer_main",
    )(h2, st, stats, u_b, vt_b)


def _final_kernel(x_ref, p_ref, mod_ref, g_ref, o_ref):
    b = pl.program_id(0)
    x = x_ref[0] + mod_ref[b][5:6, :] * p_ref[0]
    o_ref[0] = _rms(x) * g_ref[...]


def final_norm(x, p, mod, g, nbatch):
    B, L, D = x.shape
    blk = pl.BlockSpec((1, SUB, D), lambda b, n: (b, n + 1, 0))
    return pl.pallas_call(
        _final_kernel,
        grid=(B, L // SUB - 1),
        in_specs=[blk, blk, pl.BlockSpec((nbatch + 1, SUBLANES, D), lambda b, n: (0, 0, 0)),
                  pl.BlockSpec((1, D), lambda b, n: (0, 0))],
        out_specs=pl.BlockSpec((1, SUB, D), lambda b, n: (b, n, 0)),
        out_shape=jax.ShapeDtypeStruct((B, L - SUB, D), F32),
        compiler_params=_params("parallel", "parallel"),
        name="final_norm",
    )(x, p, mod, g.reshape(1, D))


def kernel(x, c, ctx, c_ctx, w_mod, b_mod, norm1_g, norm2_g, w_in, b_gate, conv_w, conv_b,
           conv_ln_g, conv_ln_b, w_conv_o, gla_wa2, gla_ba, gla_norm_g, w_gla_o, na_rpb, w_na_o,
           w_out, peer_wq, peer_keys, peer_u, peer_v, final_g):
    B, S, D = x.shape
    L = ctx.shape[1] + S
    assert ctx.shape[1] == SUB and S % SUB == 0 and (B * L) % PEER_TT == 0 and (B * L) % ROW_TILE == 0
    nb = L // SUB
    R = B * L
    xs = jnp.concatenate([ctx, x], axis=1).reshape(R, D)
    p = None
    mod_prev = None
    for i in range(DEPTH):
        mod = mod_table(c, c_ctx, w_mod[i], b_mod[i])
        proj, xs = in_projection(xs, p, mod_prev, mod, norm1_g[i], reorder_w_in(w_in[i]), nb, B)
        proj3 = proj.reshape(B, L, IN_COLS)
        conv_act = conv_branch(proj3, conv_w[i], conv_b[i], conv_ln_g[i], conv_ln_b[i], nb)
        o_f, o_b = gla_branch(proj3, gla_wa2[i], gla_ba[i], nb)
        y_na = na_branch(proj3, na_rpb[i], nb)
        xs, h2 = merge_branches(conv_act.reshape(R, CONV_DIM), o_f.reshape(R, GLA_V), o_b.reshape(R, GLA_V),
                                y_na.reshape(R, NA_DIM), proj, xs, mod, w_conv_o[i], w_gla_o[i], w_na_o[i],
                                w_out[i], b_gate[i], gla_norm_g[i], norm2_g[i], nb, B)
        p = peer_ffn(h2, peer_wq[i], peer_keys[i], peer_u[i], peer_v[i])
        mod_prev = mod
    return final_norm(xs.reshape(B, L, D), p.reshape(B, L, D), mod_prev, final_g, B)
```

```python
import functools
import math
import jax
import jax.numpy as jnp
from jax import lax
import numpy as np
from jax.experimental import pallas as pl
from jax.experimental.pallas import tpu as pltpu

D_MODEL = 1024
BATCH = 8
SEQ = 4096
DEPTH = 2
GRID_W = 64
CTX_LEN = 256
EPS = 1e-6
N_BRANCH = 3
F32 = jnp.float32
BF16 = jnp.bfloat16
CONV_DIM = 512
CONV_WIDTH = 31
GLA_HEADS = 4
GLA_DK = 64
GLA_DV = 128
GLA_RANK = 16
GLA_GATE_NORM = 16.0
GLA_CHUNK = 64
GLA_QK = GLA_HEADS * GLA_DK
GLA_V = GLA_HEADS * GLA_DV
ROPE_BASE = 10000.0
NA_HEADS = 8
NA_HD = 64
NA_DIM = NA_HEADS * NA_HD
NA_KH = 8
NA_KW = 16
PEER_HEADS = 8
PEER_NKEYS = 128
PEER_TOPK = 16
PEER_DKEY = 256

LANES = 128
SUBLANES = 8
NT_DIMS = (((1,), (1,)), ((), ()))
VMEM_LIMIT = 56 * 1024 * 1024

SUB = CTX_LEN
ROW_TILE = 512
PEER_TT = 1024
PEER_EB = 1024
PEER_CHUNK = 256
NA_QROWS = 4
NA_KROWS = 12

COL_BR = 0
COL_GLU = COL_BR + N_BRANCH * D_MODEL
COL_V = COL_GLU + 2 * CONV_DIM
COL_G = COL_V + GLA_V
COL_NQ = COL_G + GLA_V
COL_NK = COL_NQ + NA_DIM
COL_NV = COL_NK + NA_DIM
COL_Q = COL_NV + NA_DIM
COL_K = COL_Q + GLA_QK
COL_A = COL_K + GLA_QK
IN_COLS = COL_A + LANES
IN_COL_TILE = IN_COLS // 3
REF_SPLITS = (2 * CONV_DIM, GLA_QK, GLA_QK, GLA_V, GLA_V, 2 * GLA_RANK, NA_DIM, NA_DIM, NA_DIM, N_BRANCH * D_MODEL)


def _params(*sem):
    return pltpu.CompilerParams(dimension_semantics=sem, vmem_limit_bytes=VMEM_LIMIT)


def _mod_index(block, nb, nbatch):
    return jnp.where(block % nb == 0, nbatch, block // nb)


def _sigmoid(x):
    return 1.0 / (1.0 + jnp.exp(-x))


def _rms(x):
    return x * lax.rsqrt(jnp.mean(x * x, axis=-1, keepdims=True) + EPS)


def _mod_kernel(c_ref, w_ref, b_ref, o_ref):
    c = c_ref[...]
    cond = (c * _sigmoid(c)).astype(BF16)
    o_ref[...] = jnp.dot(cond, w_ref[...].astype(BF16), preferred_element_type=F32) + b_ref[...]


def mod_table(c, c_ctx, w_mod, b_mod):
    nb = c.shape[0]
    cond = jnp.concatenate([c, c_ctx[None], jnp.zeros((2 * SUBLANES - nb - 1, D_MODEL), F32)], axis=0)
    n = 6 * D_MODEL
    tn = n // 4
    out = pl.pallas_call(
        _mod_kernel,
        grid=(n // tn,),
        in_specs=[pl.BlockSpec((2 * SUBLANES, D_MODEL), lambda j: (0, 0)),
                  pl.BlockSpec((D_MODEL, tn), lambda j: (0, j)),
                  pl.BlockSpec((1, tn), lambda j: (0, j))],
        out_specs=pl.BlockSpec((2 * SUBLANES, tn), lambda j: (0, j)),
        out_shape=jax.ShapeDtypeStruct((2 * SUBLANES, n), F32),
        compiler_params=_params("parallel"),
        name="mod_table",
    )(cond, w_mod, b_mod.reshape(1, n))
    tab = out[:nb + 1].reshape(nb + 1, 6, D_MODEL)
    return jnp.concatenate([tab, jnp.zeros((nb + 1, 2, D_MODEL), F32)], axis=1)


def _inproj_kernel(*refs, has_resid, nb, nbatch):
    if has_resid:
        x_ref, p_ref, modp_ref, mod_ref, g_ref, w_ref, o_ref, xo_ref, h_ref = refs
    else:
        x_ref, mod_ref, g_ref, w_ref, o_ref, h_ref = refs
    r = pl.program_id(0)

    @pl.when(pl.program_id(1) == 0)
    def _():
        for s in range(ROW_TILE // SUB):
            rows = slice(s * SUB, (s + 1) * SUB)
            idx = _mod_index(r * (ROW_TILE // SUB) + s, nb, nbatch)
            x = x_ref[rows, :]
            if has_resid:
                x = x + modp_ref[idx][5:6, :] * p_ref[rows, :]
                xo_ref[rows, :] = x
            m = mod_ref[idx]
            h_ref[rows, :] = (_rms(x) * g_ref[...] * (1.0 + m[1:2, :]) + m[0:1, :]).astype(BF16)

    o_ref[...] = jnp.dot(h_ref[...], w_ref[...], preferred_element_type=F32)


def in_projection(x, p, mod_prev, mod, norm_g, w_in_cols, nb, nbatch):
    R = x.shape[0]
    has_resid = p is not None
    row = pl.BlockSpec((ROW_TILE, D_MODEL), lambda r, j: (r, 0))
    tab = pl.BlockSpec((nbatch + 1, SUBLANES, D_MODEL), lambda r, j: (0, 0, 0))
    in_specs = [row] + ([row, tab] if has_resid else []) + [
        tab, pl.BlockSpec((1, D_MODEL), lambda r, j: (0, 0)),
        pl.BlockSpec((D_MODEL, IN_COL_TILE), lambda r, j: (0, j))]
    out_specs = [pl.BlockSpec((ROW_TILE, IN_COL_TILE), lambda r, j: (r, j))] + ([row] if has_resid else [])
    out_shape = [jax.ShapeDtypeStruct((R, IN_COLS), F32)] + (
        [jax.ShapeDtypeStruct((R, D_MODEL), F32)] if has_resid else [])
    args = [x] + ([p, mod_prev] if has_resid else []) + [mod, norm_g.reshape(1, D_MODEL), w_in_cols]
    outs = pl.pallas_call(
        functools.partial(_inproj_kernel, has_resid=has_resid, nb=nb, nbatch=nbatch),
        grid=(R // ROW_TILE, IN_COLS // IN_COL_TILE),
        in_specs=in_specs, out_specs=out_specs, out_shape=out_shape,
        scratch_shapes=[pltpu.VMEM((ROW_TILE, D_MODEL), BF16)],
        compiler_params=_params("parallel", "arbitrary"),
        name="in_projection",
    )(*args)
    return (outs[0], outs[1]) if has_resid else (outs[0], x)


def reorder_w_in(w_in):
    parts = dict(zip(("glu", "q", "k", "v", "g", "a", "nq", "nk", "nv", "br"),
                     jnp.split(w_in, np.cumsum(REF_SPLITS)[:-1].tolist(), axis=-1)))
    a_pad = jnp.zeros((D_MODEL, LANES - 2 * GLA_RANK), w_in.dtype)
    cols = [parts[n] for n in ("br", "glu", "v", "g", "nq", "nk", "nv", "q", "k", "a")] + [a_pad]
    return jnp.concatenate(cols, axis=-1).astype(BF16)


def _conv_kernel(prev_ref, cur_ref, next_ref, w_ref, b_ref, lg_ref, lb_ref, o_ref, ext_ref, *, nb):
    n = pl.program_id(1)
    half = CONV_WIDTH // 2
    pad = 2 * SUBLANES

    def glu(x):
        return x[:, :CONV_DIM] * _sigmoid(x[:, CONV_DIM:])

    has_prev = n >= 2
    has_next = (n >= 1) & (n <= nb - 2)
    ext_ref[0:pad, :] = jnp.where(has_prev, glu(prev_ref[0, SUB - pad:SUB, :]), 0.0)
    ext_ref[pad:pad + SUB, :] = glu(cur_ref[0])
    ext_ref[pad + SUB:pad + SUB + pad, :] = jnp.where(has_next, glu(next_ref[0, 0:pad, :]), 0.0)
    acc = jnp.zeros((SUB, CONV_DIM), F32)
    for j in range(CONV_WIDTH):
        off = pad - half + j
        acc = acc + ext_ref[off:off + SUB, :] * w_ref[j:j + 1, :]
    u = acc + b_ref[...]
    mu = jnp.mean(u, axis=-1, keepdims=True)
    uc = u - mu
    var = jnp.mean(uc * uc, axis=-1, keepdims=True)
    y = uc * lax.rsqrt(var + EPS) * lg_ref[...] + lb_ref[...]
    o_ref[0] = (y * _sigmoid(y)).astype(BF16)


def conv_branch(proj, conv_w, conv_b, ln_g, ln_b, nb):
    B, L, _ = proj.shape
    cb = COL_GLU // (2 * CONV_DIM)
    blk = (1, SUB, 2 * CONV_DIM)
    vec = pl.BlockSpec((1, CONV_DIM), lambda b, n: (0, 0))
    return pl.pallas_call(
        functools.partial(_conv_kernel, nb=nb),
        grid=(B, nb),
        in_specs=[pl.BlockSpec(blk, lambda b, n: (b, jnp.maximum(n - 1, 0), cb)),
                  pl.BlockSpec(blk, lambda b, n: (b, n, cb)),
                  pl.BlockSpec(blk, lambda b, n: (b, jnp.minimum(n + 1, nb - 1), cb)),
                  pl.BlockSpec((CONV_WIDTH + 1, CONV_DIM), lambda b, n: (0, 0)), vec, vec, vec],
        out_specs=pl.BlockSpec((1, SUB, CONV_DIM), lambda b, n: (b, n, 0)),
        out_shape=jax.ShapeDtypeStruct((B, L, CONV_DIM), BF16),
        scratch_shapes=[pltpu.VMEM((SUB + 4 * SUBLANES, CONV_DIM), F32)],
        compiler_params=_params("parallel", "parallel"),
        name="conv_branch",
    )(proj, proj, proj, jnp.concatenate([conv_w, jnp.zeros((1, CONV_DIM), F32)], axis=0),
      conv_b.reshape(1, CONV_DIM), ln_g.reshape(1, CONV_DIM), ln_b.reshape(1, CONV_DIM))


def gla_rope_tables(n_ctx, n_lat):
    t = jnp.arange(n_lat)
    row = (t // GRID_W).astype(F32)
    col = (t % GRID_W).astype(F32)
    n_freq = GLA_DK // 4
    inv = ROPE_BASE ** (-jnp.arange(n_freq, dtype=F32) / n_freq)
    ang = jnp.concatenate([row[:, None] * inv, col[:, None] * inv], axis=-1)
    cos = jnp.repeat(jnp.cos(ang), 2, axis=-1)
    sin = jnp.repeat(jnp.sin(ang), 2, axis=-1)
    sign = jnp.tile(jnp.array([-1.0, 1.0], F32), GLA_DK // 2)
    cos = jnp.tile(cos, (1, GLA_HEADS))
    sin = jnp.tile(sin * sign, (1, GLA_HEADS))
    cos = jnp.concatenate([jnp.ones((n_ctx, GLA_QK), F32), cos], axis=0)
    sin = jnp.concatenate([jnp.zeros((n_ctx, GLA_QK), F32), sin], axis=0)
    return cos, sin


def _swap_pairs(x):
    lane = lax.broadcasted_iota(jnp.int32, x.shape, 1)
    return jnp.where(lane % 2 == 0, pltpu.roll(x, LANES - 1, axis=1), pltpu.roll(x, 1, axis=1))


def _rope(x, cos, sin):
    return jnp.concatenate([x[:, p * LANES:(p + 1) * LANES] * cos[:, p * LANES:(p + 1) * LANES]
                            + _swap_pairs(x[:, p * LANES:(p + 1) * LANES]) * sin[:, p * LANES:(p + 1) * LANES]
                            for p in range(GLA_QK // LANES)], axis=1)


def _split3(x):
    hi = x.astype(BF16)
    r = x - hi.astype(F32)
    mid = r.astype(BF16)
    lo = (r - mid.astype(F32)).astype(BF16)
    return hi, mid, lo


def _gla_chunk(q, k, v, a, cos, sin, wa, ba, state_ref, backward):
    C = GLA_CHUNK
    qr = _rope(q * np.float32(GLA_DK ** -0.5), cos, sin)
    kr = _rope(k, cos, sin)
    z = jnp.dot(a.astype(BF16), wa, preferred_element_type=F32) + ba
    la = (jnp.minimum(z, 0.0) - jnp.log(1.0 + jnp.exp(-jnp.abs(z)))) * np.float32(1.0 / GLA_GATE_NORM)
    ri = lax.broadcasted_iota(jnp.int32, (C, C), 0)
    ci = lax.broadcasted_iota(jnp.int32, (C, C), 1)
    keep = (ci >= ri) if backward else (ci <= ri)
    tri = jnp.where(keep, 1.0, 0.0).astype(BF16)
    hi, mid, lo = _split3(la)
    b = (jnp.dot(tri, hi, preferred_element_type=F32) + jnp.dot(tri, mid, preferred_element_type=F32)
         + jnp.dot(tri, lo, preferred_element_type=F32))
    mid_row = C // 2 - 1 if backward else C // 2
    end_row = 0 if backward else C - 1
    b_mid = b[mid_row:mid_row + 1]
    b_end = b[end_row:end_row + 1]
    qe = qr * jnp.exp(b)
    qa = qr * jnp.exp(b - b_mid)
    ka = kr * jnp.exp(b_mid - b)
    ke = kr * jnp.exp(b_end - b)
    decay = jnp.exp(b_end)
    lane = lax.broadcasted_iota(jnp.int32, (C, LANES), 1)
    row128 = lax.broadcasted_iota(jnp.int32, (LANES, LANES), 0)
    outs = []
    for p in range(GLA_QK // LANES):
        ps = slice(p * LANES, (p + 1) * LANES)
        s_prev = state_ref[p]
        s_b = s_prev.astype(BF16)
        ka_p = ka[:, ps].astype(BF16)
        ke_t = ke[:, ps].T.astype(BF16)
        upd = []
        for hh in range(2):
            h = 2 * p + hh
            in_head = (lane >= hh * GLA_DK) & (lane < (hh + 1) * GLA_DK)
            v_h = v[:, h * GLA_DV:(h + 1) * GLA_DV].astype(BF16)
            att = lax.dot_general(jnp.where(in_head, qa[:, ps], 0.0).astype(BF16), ka_p, NT_DIMS,
                                  preferred_element_type=F32)
            att = jnp.where(keep, att, 0.0)
            o_h = (jnp.dot(att.astype(BF16), v_h, preferred_element_type=F32)
                   + jnp.dot(jnp.where(in_head, qe[:, ps], 0.0).astype(BF16), s_b, preferred_element_type=F32))
            outs.append(o_h)
            upd.append(jnp.dot(ke_t, v_h, preferred_element_type=F32))
        d_col = jnp.broadcast_to(decay[:, ps], (LANES, LANES)).T
        state_ref[p] = s_prev * d_col + jnp.where(row128 < GLA_DK, upd[0], upd[1])
    return jnp.concatenate(outs, axis=1)


def _gla_kernel(qf_ref, kf_ref, vf_ref, af_ref, cosf_ref, sinf_ref,
                qb_ref, kb_ref, vb_ref, ab_ref, cosb_ref, sinb_ref,
                wa_ref, ba_ref, of_ref, ob_ref, sf_ref, sb_ref):
    n = pl.program_id(1)

    @pl.when(n == 0)
    def _():
        sf_ref[...] = jnp.zeros(sf_ref.shape, F32)
        sb_ref[...] = jnp.zeros(sb_ref.shape, F32)

    nch = SUB // GLA_CHUNK
    for j in range(nch):
        rf = slice(j * GLA_CHUNK, (j + 1) * GLA_CHUNK)
        of_ref[0, rf, :] = _gla_chunk(qf_ref[0, rf, :], kf_ref[0, rf, :], vf_ref[0, rf, :], af_ref[0, rf, :],
                                      cosf_ref[rf, :], sinf_ref[rf, :], wa_ref[0], ba_ref[0:1, :], sf_ref, False)
        jb = nch - 1 - j
        rb = slice(jb * GLA_CHUNK, (jb + 1) * GLA_CHUNK)
        ob_ref[0, rb, :] = _gla_chunk(qb_ref[0, rb, :], kb_ref[0, rb, :], vb_ref[0, rb, :], ab_ref[0, rb, :],
                                      cosb_ref[rb, :], sinb_ref[rb, :], wa_ref[1], ba_ref[1:2, :], sb_ref, True)


def gla_branch(proj, wa2, ba, nb):
    B, L, _ = proj.shape
    cos, sin = gla_rope_tables(SUB, L - SUB)
    wa = jnp.zeros((2, LANES, GLA_QK), F32)
    wa = wa.at[0, :GLA_RANK].set(wa2[0]).at[1, GLA_RANK:2 * GLA_RANK].set(wa2[1]).astype(BF16)

    def fwd(b, n):
        return (b, n)

    def bwd(b, n):
        return (b, jnp.where(n == 0, 0, nb - n))

    def specs(f):
        def col(width, offset):
            return pl.BlockSpec((1, SUB, width), lambda b, n: f(b, n) + (offset // width,))
        tab = pl.BlockSpec((SUB, GLA_QK), lambda b, n: (f(b, n)[1], 0))
        return [col(GLA_QK, COL_Q), col(GLA_QK, COL_K), col(GLA_V, COL_V), col(LANES, COL_A), tab, tab]

    return pl.pallas_call(
        _gla_kernel,
        grid=(B, nb),
        in_specs=specs(fwd) + specs(bwd) + [pl.BlockSpec((2, LANES, GLA_QK), lambda b, n: (0, 0, 0)),
                                            pl.BlockSpec((2, GLA_QK), lambda b, n: (0, 0))],
        out_specs=[pl.BlockSpec((1, SUB, GLA_V), lambda b, n: fwd(b, n) + (0,)),
                   pl.BlockSpec((1, SUB, GLA_V), lambda b, n: bwd(b, n) + (0,))],
        out_shape=[jax.ShapeDtypeStruct((B, L, GLA_V), F32), jax.ShapeDtypeStruct((B, L, GLA_V), F32)],
        scratch_shapes=[pltpu.VMEM((GLA_QK // LANES, LANES, GLA_DV), F32),
                        pltpu.VMEM((GLA_QK // LANES, LANES, GLA_DV), F32)],
        compiler_params=_params("parallel", "arbitrary"),
        name="gla_scan",
    )(proj, proj, proj, proj, cos, sin, proj, proj, proj, proj, cos, sin, wa, ba)


def na_key_row_start(x, rows):
    return min(max(NA_QROWS * x - NA_KH // 2, 0), rows - NA_KROWS)


def na_bias_blocks(rpb, rows):
    W = GRID_W
    rs = np.clip(np.arange(rows) - NA_KH // 2, 0, rows - NA_KH)
    cs = np.clip(np.arange(W) - NA_KW // 2, 0, W - NA_KW)
    nrb = rows // NA_QROWS
    blocks = (0, 1, nrb - 1)
    qr = np.stack([np.arange(NA_QROWS) + NA_QROWS * x for x in blocks])
    kr = np.stack([na_key_row_start(x, rows) + np.arange(NA_KROWS) for x in blocks])
    ok_r = (kr[:, None, :] >= rs[qr][:, :, None]) & (kr[:, None, :] < rs[qr][:, :, None] + NA_KH)
    i_r = np.clip(kr[:, None, :] - qr[:, :, None] + NA_KH - 1, 0, 2 * NA_KH - 2)
    qc = np.arange(W)
    kc = np.arange(W)
    ok_c = (kc[None, :] >= cs[qc][:, None]) & (kc[None, :] < cs[qc][:, None] + NA_KW)
    i_c = np.clip(kc[None, :] - qc[:, None] + NA_KW - 1, 0, 2 * NA_KW - 2)
    ok = ok_r[:, :, None, :, None] & ok_c[None, None, :, None, :]
    pick_r = (i_r[..., None] == np.arange(2 * NA_KH - 1)).astype(np.float32)
    pick_c = (i_c[..., None] == np.arange(2 * NA_KW - 1)).astype(np.float32)
    rows_sel = jnp.einsum('tajr,hrc->htajc', pick_r, rpb.astype(F32), precision=lax.Precision.HIGHEST)
    vals = jnp.einsum('htajc,qkc->htaqjk', rows_sel, pick_c, precision=lax.Precision.HIGHEST)
    bias = jnp.where(ok[None], vals, -jnp.inf)
    return bias.reshape(rpb.shape[0], 3, NA_QROWS * W, NA_KROWS * W)


def _na_kernel(q_ref, k_ref, v_ref, bias_ref, o_ref, *, rows):
    n = pl.program_id(2)
    W = GRID_W
    q = q_ref[0] * np.float32(NA_HD ** -0.5)
    kc = k_ref[0, 0:SUB, :].astype(BF16)
    vc = v_ref[0, 0:SUB, :].astype(BF16)
    lane = lax.broadcasted_iota(jnp.int32, (SUB, LANES), 1)

    def head_q(hh):
        in_head = (lane >= hh * NA_HD) & (lane < (hh + 1) * NA_HD)
        return jnp.where(in_head, q, 0.0).astype(BF16)

    @pl.when(n == 0)
    def _():
        outs = []
        for hh in range(2):
            sc = lax.dot_general(head_q(hh), kc, NT_DIMS, preferred_element_type=F32)
            pc = jnp.exp(sc - jnp.max(sc, axis=-1, keepdims=True))
            o = jnp.dot(pc.astype(BF16), vc, preferred_element_type=F32)
            outs.append(o / jnp.sum(pc, axis=-1, keepdims=True))
        o_ref[0] = jnp.where(lane < NA_HD, outs[0], outs[1]).astype(BF16)

    @pl.when(n > 0)
    def _():
        start_row = jnp.minimum(jnp.maximum(NA_QROWS * (n - 1) - NA_KH // 2, 0), rows - NA_KROWS)
        win = pl.ds(pl.multiple_of(SUB + start_row * W, W), NA_KROWS * W)
        kw = k_ref[0, win, :].astype(BF16)
        vw = v_ref[0, win, :].astype(BF16)
        outs = []
        for hh in range(2):
            qm = head_q(hh)
            s = lax.dot_general(qm, kw, NT_DIMS, preferred_element_type=F32) + bias_ref[hh, 0]
            sc = lax.dot_general(qm, kc, NT_DIMS, preferred_element_type=F32)
            m = jnp.maximum(jnp.max(s, axis=-1, keepdims=True), jnp.max(sc, axis=-1, keepdims=True))
            p = jnp.exp(s - m)
            pc = jnp.exp(sc - m)
            l = jnp.sum(p, axis=-1, keepdims=True) + jnp.sum(pc, axis=-1, keepdims=True)
            o = (jnp.dot(p.astype(BF16), vw, preferred_element_type=F32)
                 + jnp.dot(pc.astype(BF16), vc, preferred_element_type=F32))
            outs.append(o / l)
        o_ref[0] = jnp.where(lane < NA_HD, outs[0], outs[1]).astype(BF16)


def na_branch(proj, rpb, nb):
    B, L, _ = proj.shape
    rows = (L - SUB) // GRID_W
    nrb = nb - 1
    bias = na_bias_blocks(rpb, rows)

    def bias_idx(p, b, n):
        return (p, jnp.where(n <= 1, 0, jnp.where(n == nrb, 2, 1)), 0, 0)

    return pl.pallas_call(
        functools.partial(_na_kernel, rows=rows),
        grid=(NA_DIM // LANES, B, nb),
        in_specs=[pl.BlockSpec((1, SUB, LANES), lambda p, b, n: (b, n, COL_NQ // LANES + p)),
                  pl.BlockSpec((1, L, LANES), lambda p, b, n: (b, 0, COL_NK // LANES + p)),
                  pl.BlockSpec((1, L, LANES), lambda p, b, n: (b, 0, COL_NV // LANES + p)),
                  pl.BlockSpec((2, 1, NA_QROWS * GRID_W, NA_KROWS * GRID_W), bias_idx)],
        out_specs=pl.BlockSpec((1, SUB, LANES), lambda p, b, n: (b, n, p)),
        out_shape=jax.ShapeDtypeStruct((B, L, NA_DIM), BF16),
        compiler_params=_params("parallel", "parallel", "arbitrary"),
        name="na_attention",
    )(proj, proj, proj, bias)


def _merge_kernel(conv_ref, of_ref, ob_ref, pg_ref, na_ref, br_ref, x_ref, mod_ref, wc_ref, wg_ref, wn_ref,
                  wo_ref, bg_ref, gn_ref, n2_ref, xo_ref, h2_ref, *, nb, nbatch):
    r = pl.program_id(0)
    o = of_ref[...] + ob_ref[...]
    on = jnp.concatenate([_rms(o[:, h * GLA_DV:(h + 1) * GLA_DV]) for h in range(GLA_HEADS)], axis=1) * gn_ref[...]
    pg = pg_ref[...]
    y_gla = jnp.dot((on * (pg * _sigmoid(pg))).astype(BF16), wg_ref[...], preferred_element_type=F32)
    y_conv = jnp.dot(conv_ref[...], wc_ref[...], preferred_element_type=F32)
    y_na = jnp.dot(na_ref[...], wn_ref[...], preferred_element_type=F32)
    m = (_sigmoid(br_ref[:, 0:D_MODEL] + bg_ref[:, 0:D_MODEL]) * y_conv
         + _sigmoid(br_ref[:, D_MODEL:2 * D_MODEL] + bg_ref[:, D_MODEL:2 * D_MODEL]) * y_gla
         + _sigmoid(br_ref[:, 2 * D_MODEL:3 * D_MODEL] + bg_ref[:, 2 * D_MODEL:3 * D_MODEL]) * y_na)
    y = jnp.dot(m.astype(BF16), wo_ref[...], preferred_element_type=F32)
    for s in range(ROW_TILE // SUB):
        rows = slice(s * SUB, (s + 1) * SUB)
        md = mod_ref[_mod_index(r * (ROW_TILE // SUB) + s, nb, nbatch)]
        x = x_ref[rows, :] + md[2:3, :] * y[rows, :]
        xo_ref[rows, :] = x
        h2_ref[rows, :] = (_rms(x) * n2_ref[...] * (1.0 + md[4:5, :]) + md[3:4, :]).astype(BF16)


def merge_branches(conv_act, o_f, o_b, y_na, proj, x, mod, w_conv_o, w_gla_o, w_na_o, w_out, b_gate,
                   gla_norm_g, norm2_g, nb, nbatch):
    R = x.shape[0]

    def rows(width, block=0):
        return pl.BlockSpec((ROW_TILE, width), lambda r: (r, block))

    def whole(shape):
        return pl.BlockSpec(shape, lambda r: (0,) * len(shape))

    return pl.pallas_call(
        functools.partial(_merge_kernel, nb=nb, nbatch=nbatch),
        grid=(R // ROW_TILE,),
        in_specs=[rows(CONV_DIM), rows(GLA_V), rows(GLA_V), rows(GLA_V, COL_G // GLA_V), rows(NA_DIM),
                  rows(N_BRANCH * D_MODEL, COL_BR // (N_BRANCH * D_MODEL)), rows(D_MODEL),
                  whole((nbatch + 1, SUBLANES, D_MODEL)),
                  whole((CONV_DIM, D_MODEL)), whole((GLA_V, D_MODEL)), whole((NA_DIM, D_MODEL)),
                  whole((D_MODEL, D_MODEL)), whole((1, N_BRANCH * D_MODEL)), whole((1, GLA_V)),
                  whole((1, D_MODEL))],
        out_specs=[rows(D_MODEL), rows(D_MODEL)],
        out_shape=[jax.ShapeDtypeStruct((R, D_MODEL), F32), jax.ShapeDtypeStruct((R, D_MODEL), BF16)],
        compiler_params=_params("parallel"),
        name="merge_branches",
    )(conv_act, o_f, o_b, proj, y_na, proj, x, mod, w_conv_o.astype(BF16), w_gla_o.astype(BF16),
      w_na_o.astype(BF16), w_out.astype(BF16), b_gate.reshape(1, -1), gla_norm_g.reshape(1, -1),
      norm2_g.reshape(1, -1))


def _peer_route_kernel(h_ref, wq_ref, keys_ref, st_ref, stats_ref, *, nh, topk):
    tt = h_ref.shape[0]
    qry = jnp.dot(h_ref[...], wq_ref[...], preferred_element_type=F32)
    neg = jnp.float32(-jnp.inf)
    for h in range(nh):
        tops = []
        for z in range(2):
            hz = 2 * h + z
            q = qry[:, hz * LANES:(hz + 1) * LANES].astype(BF16)
            s = lax.dot_general(keys_ref[hz], q, NT_DIMS, preferred_element_type=F32)
            st_ref[hz] = s
            cur = s
            vals = []
            for _ in range(topk):
                m = jnp.max(cur, axis=0, keepdims=True)
                vals.append(m)
                cur = jnp.where(cur == m, neg, cur)
            tops.append(vals)
        a, b = tops
        cands = [a[p] + b[q] for p in range(topk) for q in range(topk) if (p + 1) * (q + 1) <= topk]
        npad = (-len(cands)) % SUBLANES
        cand = jnp.concatenate(cands + [jnp.full((npad, tt), neg, F32)], axis=0)
        cur = cand
        tau = None
        for _ in range(topk):
            tau = jnp.max(cur, axis=0, keepdims=True)
            cur = jnp.where(cur == tau, neg, cur)
        top = a[0] + b[0]
        zsum = jnp.sum(jnp.where(cand >= tau, jnp.exp(cand - top), 0.0), axis=0, keepdims=True)
        stats_ref[h] = jnp.concatenate([tau, a[0], 1.0 / zsum, jnp.zeros((SUBLANES - 3, tt), F32)] + b, axis=0)


def _peer_main_kernel(hb_ref, st_ref, stats_ref, u_ref, vt_ref, out_ref, acc_ref, e0_ref, cnt_ref, e1_ref, rank_ref,
                      *, nh, eb, tt, topk):
    i = pl.program_id(1)

    @pl.when(i == 0)
    def _():
        acc_ref[...] = jnp.zeros(acc_ref.shape, F32)
        for h in range(nh):
            tau = stats_ref[h, 0:1, :]
            a1 = stats_ref[h, 1:2, :]
            rz = stats_ref[h, 2:3, :]
            s0 = st_ref[2 * h]
            s1 = st_ref[2 * h + 1]
            e0_ref[h] = jnp.exp(s0 - a1) * rz
            e1_ref[h] = jnp.exp(s1 - stats_ref[h, SUBLANES:SUBLANES + 1, :]).astype(BF16)
            cnt = jnp.zeros(s0.shape, F32)
            rank = jnp.zeros(s1.shape, F32)
            for q in range(topk):
                bq = stats_ref[h, SUBLANES + q:SUBLANES + q + 1, :]
                cnt = cnt + jnp.where(s0 + bq >= tau, 1.0, 0.0)
                rank = rank + jnp.where(bq > s1, 1.0, 0.0)
            cnt_ref[h] = cnt
            rank_ref[h] = rank.astype(BF16)

    nsub = eb // LANES
    assert nsub == SUBLANES
    first = pl.ds(pl.multiple_of(i * SUBLANES, SUBLANES), SUBLANES)
    for c in range(tt // PEER_CHUNK):
        cs = slice(c * PEER_CHUNK, (c + 1) * PEER_CHUNK)
        act = lax.dot_general(u_ref[...], hb_ref[cs, :], NT_DIMS, preferred_element_type=F32)
        e0 = [e0_ref[h, first, cs] for h in range(nh)]
        cnt = [cnt_ref[h, first, cs] for h in range(nh)]
        ws = []
        for s in range(nsub):
            g = jnp.zeros((LANES, PEER_CHUNK), BF16)
            for h in range(nh):
                keep = rank_ref[h, :, cs] < cnt[h][s:s + 1].astype(BF16)
                g = g + jnp.where(keep, e1_ref[h, :, cs] * e0[h][s:s + 1].astype(BF16), jnp.zeros((), BF16))
            a = act[s * LANES:(s + 1) * LANES]
            ge = 0.5 * a * (1.0 + lax.erf(a * np.float32(math.sqrt(0.5))))
            ws.append(g * ge.astype(BF16))
        w = jnp.concatenate(ws, axis=0)
        acc_ref[:, cs] += jnp.dot(vt_ref[...], w, preferred_element_type=F32)

    @pl.when(i == pl.num_programs(1) - 1)
    def _():
        out_ref[...] = acc_ref[...].T


def peer_ffn(h2, wq, keys, u_tab, v_tab):
    T, D = h2.shape
    tt, eb, topk = PEER_TT, PEER_EB, PEER_TOPK
    nh = keys.shape[0]
    nk = keys.shape[2]
    assert nk == LANES and keys.shape[3] == LANES and T % tt == 0
    E = nk * nk
    wq_b = wq.astype(BF16)
    keys_b = keys.reshape(2 * nh, nk, LANES).astype(BF16)
    nt = T // tt
    st, stats = pl.pallas_call(
        functools.partial(_peer_route_kernel, nh=nh, topk=topk),
        grid=(nt,),
        in_specs=[pl.BlockSpec((tt, D), lambda t: (t, 0)),
                  pl.BlockSpec((D, 2 * nh * LANES), lambda t: (0, 0)),
                  pl.BlockSpec((2 * nh, nk, LANES), lambda t: (0, 0, 0))],
        out_specs=[pl.BlockSpec((2 * nh, nk, tt), lambda t: (0, 0, t)),
                   pl.BlockSpec((nh, SUBLANES + topk, tt), lambda t: (0, 0, t))],
        out_shape=[jax.ShapeDtypeStruct((2 * nh, nk, T), F32),
                   jax.ShapeDtypeStruct((nh, SUBLANES + topk, T), F32)],
        compiler_params=_params("parallel"),
        name="peer_route",
    )(h2, wq_b, keys_b)
    u_b = u_tab.astype(BF16)
    vt_b = v_tab.T.astype(BF16)
    return pl.pallas_call(
        functools.partial(_peer_main_kernel, nh=nh, eb=eb, tt=tt, topk=topk),
        grid=(nt, E // eb),
        in_specs=[pl.BlockSpec((tt, D), lambda t, i: (t, 0), pipeline_mode=pl.Buffered(1)),
                  pl.BlockSpec((2 * nh, nk, tt), lambda t, i: (0, 0, t), pipeline_mode=pl.Buffered(1)),
                  pl.BlockSpec((nh, SUBLANES + topk, tt), lambda t, i: (0, 0, t)),
                  pl.BlockSpec((eb, D), lambda t, i: (i, 0)),
                  pl.BlockSpec((D, eb), lambda t, i: (0, i))],
        out_specs=pl.BlockSpec((tt, D), lambda t, i: (t, 0)),
        out_shape=jax.ShapeDtypeStruct((T, D), F32),
        scratch_shapes=[pltpu.VMEM((D, tt), F32),
                        pltpu.VMEM((nh, nk, tt), F32),
                        pltpu.VMEM((nh, nk, tt), F32),
                        pltpu.VMEM((nh, nk, tt), BF16),
                        pltpu.VMEM((nh, nk, tt), BF16)],
        compiler_params=_params("parallel", "arbitrary"),
        name="peer_main",
    )(h2, st, stats, u_b, vt_b)


def _final_kernel(x_ref, p_ref, mod_ref, g_ref, o_ref):
    b = pl.program_id(0)
    x = x_ref[0] + mod_ref[b][5:6, :] * p_ref[0]
    o_ref[0] = _rms(x) * g_ref[...]


def final_norm(x, p, mod, g, nbatch):
    B, L, D = x.shape
    blk = pl.BlockSpec((1, SUB, D), lambda b, n: (b, n + 1, 0))
    return pl.pallas_call(
        _final_kernel,
        grid=(B, L // SUB - 1),
        in_specs=[blk, blk, pl.BlockSpec((nbatch + 1, SUBLANES, D), lambda b, n: (0, 0, 0)),
                  pl.BlockSpec((1, D), lambda b, n: (0, 0))],
        out_specs=pl.BlockSpec((1, SUB, D), lambda b, n: (b, n, 0)),
        out_shape=jax.ShapeDtypeStruct((B, L - SUB, D), F32),
        compiler_params=_params("parallel", "parallel"),
        name="final_norm",
    )(x, p, mod, g.reshape(1, D))


def kernel(x, c, ctx, c_ctx, w_mod, b_mod, norm1_g, norm2_g, w_in, b_gate, conv_w, conv_b,
           conv_ln_g, conv_ln_b, w_conv_o, gla_wa2, gla_ba, gla_norm_g, w_gla_o, na_rpb, w_na_o,
           w_out, peer_wq, peer_keys, peer_u, peer_v, final_g):
    B, S, D = x.shape
    L = ctx.shape[1] + S
    assert ctx.shape[1] == SUB and S % SUB == 0 and (B * L) % PEER_TT == 0 and (B * L) % ROW_TILE == 0
    nb = L // SUB
    R = B * L
    xs = jnp.concatenate([ctx, x], axis=1).reshape(R, D)
    p = None
    mod_prev = None
    for i in range(DEPTH):
        mod = mod_table(c, c_ctx, w_mod[i], b_mod[i])
        proj, xs = in_projection(xs, p, mod_prev, mod, norm1_g[i], reorder_w_in(w_in[i]), nb, B)
        proj3 = proj.reshape(B, L, IN_COLS)
        conv_act = conv_branch(proj3, conv_w[i], conv_b[i], conv_ln_g[i], conv_ln_b[i], nb)
        o_f, o_b = gla_branch(proj3, gla_wa2[i], gla_ba[i], nb)
        y_na = na_branch(proj3, na_rpb[i], nb)
        xs, h2 = merge_branches(conv_act.reshape(R, CONV_DIM), o_f.reshape(R, GLA_V), o_b.reshape(R, GLA_V),
                                y_na.reshape(R, NA_DIM), proj, xs, mod, w_conv_o[i], w_gla_o[i], w_na_o[i],
                                w_out[i], b_gate[i], gla_norm_g[i], norm2_g[i], nb, B)
        p = peer_ffn(h2, peer_wq[i], peer_keys[i], peer_u[i], peer_v[i])
        mod_prev = mod
    return final_norm(xs.reshape(B, L, D), p.reshape(B, L, D), mod_prev, final_g, B)
```

```python
import functools
import math
import jax
import jax.numpy as jnp
from jax import lax
import numpy as np
from jax.experimental import pallas as pl
from jax.experimental.pallas import tpu as pltpu

D_MODEL = 1024
BATCH = 8
SEQ = 4096
DEPTH = 2
GRID_W = 64
CTX_LEN = 256
EPS = 1e-6
N_BRANCH = 3
F32 = jnp.float32
BF16 = jnp.bfloat16
CONV_DIM = 512
CONV_WIDTH = 31
GLA_HEADS = 4
GLA_DK = 64
GLA_DV = 128
GLA_RANK = 16
GLA_GATE_NORM = 16.0
GLA_CHUNK = 64
GLA_QK = GLA_HEADS * GLA_DK
GLA_V = GLA_HEADS * GLA_DV
ROPE_BASE = 10000.0
NA_HEADS = 8
NA_HD = 64
NA_DIM = NA_HEADS * NA_HD
NA_KH = 8
NA_KW = 16
PEER_HEADS = 8
PEER_NKEYS = 128
PEER_TOPK = 16
PEER_DKEY = 256

LANES = 128
SUBLANES = 8
NT_DIMS = (((1,), (1,)), ((), ()))
VMEM_LIMIT = 56 * 1024 * 1024

SUB = CTX_LEN
ROW_TILE = 512
INPROJ_TILE = 1024
PEER_TT = 1024
PEER_EB = 1024
PEER_CHUNK = 256
NA_QROWS = 4
NA_KROWS = 12

COL_BR = 0
COL_GLU = COL_BR + N_BRANCH * D_MODEL
COL_V = COL_GLU + 2 * CONV_DIM
COL_G = COL_V + GLA_V
COL_NQ = COL_G + GLA_V
COL_NK = COL_NQ + NA_DIM
COL_NV = COL_NK + NA_DIM
COL_Q = COL_NV + NA_DIM
COL_K = COL_Q + GLA_QK
COL_A = COL_K + GLA_QK
IN_COLS = COL_A + LANES
IN_COL_TILE = IN_COLS // 3
REF_SPLITS = (2 * CONV_DIM, GLA_QK, GLA_QK, GLA_V, GLA_V, 2 * GLA_RANK, NA_DIM, NA_DIM, NA_DIM, N_BRANCH * D_MODEL)


def _params(*sem):
    return pltpu.CompilerParams(dimension_semantics=sem, vmem_limit_bytes=VMEM_LIMIT)


def _mod_index(block, nb, nbatch):
    return jnp.where(block % nb == 0, nbatch, block // nb)


def _sigmoid(x):
    return 1.0 / (1.0 + jnp.exp(-x))


def _rms(x):
    return x * lax.rsqrt(jnp.mean(x * x, axis=-1, keepdims=True) + EPS)


def _mod_kernel(c_ref, w_ref, b_ref, o_ref):
    c = c_ref[...]
    cond = (c * _sigmoid(c)).astype(BF16)
    o_ref[...] = jnp.dot(cond, w_ref[...].astype(BF16), preferred_element_type=F32) + b_ref[...]


def mod_table(c, c_ctx, w_mod, b_mod):
    nb = c.shape[0]
    cond = jnp.concatenate([c, c_ctx[None], jnp.zeros((2 * SUBLANES - nb - 1, D_MODEL), F32)], axis=0)
    n = 6 * D_MODEL
    tn = n // 4
    out = pl.pallas_call(
        _mod_kernel,
        grid=(n // tn,),
        in_specs=[pl.BlockSpec((2 * SUBLANES, D_MODEL), lambda j: (0, 0)),
                  pl.BlockSpec((D_MODEL, tn), lambda j: (0, j)),
                  pl.BlockSpec((1, tn), lambda j: (0, j))],
        out_specs=pl.BlockSpec((2 * SUBLANES, tn), lambda j: (0, j)),
        out_shape=jax.ShapeDtypeStruct((2 * SUBLANES, n), F32),
        compiler_params=_params("parallel"),
        name="mod_table",
    )(cond, w_mod, b_mod.reshape(1, n))
    tab = out[:nb + 1].reshape(nb + 1, 6, D_MODEL)
    return jnp.concatenate([tab, jnp.zeros((nb + 1, 2, D_MODEL), F32)], axis=1)


def _inproj_kernel(*refs, has_resid, nb, nbatch):
    if has_resid:
        x_ref, p_ref, modp_ref, mod_ref, g_ref, w_ref, o_ref, xo_ref, h_ref = refs
    else:
        x_ref, mod_ref, g_ref, w_ref, o_ref, h_ref = refs
    r = pl.program_id(0)

    @pl.when(pl.program_id(1) == 0)
    def _():
        for s in range(INPROJ_TILE // SUB):
            rows = slice(s * SUB, (s + 1) * SUB)
            idx = _mod_index(r * (INPROJ_TILE // SUB) + s, nb, nbatch)
            x = x_ref[rows, :]
            if has_resid:
                x = x + modp_ref[idx][5:6, :] * p_ref[rows, :]
                xo_ref[rows, :] = x
            m = mod_ref[idx]
            h_ref[rows, :] = (_rms(x) * g_ref[...] * (1.0 + m[1:2, :]) + m[0:1, :]).astype(BF16)

    o_ref[...] = jnp.dot(h_ref[...], w_ref[...], preferred_element_type=F32).astype(BF16)


def in_projection(x, p, mod_prev, mod, norm_g, w_in_cols, nb, nbatch):
    R = x.shape[0]
    has_resid = p is not None
    row = pl.BlockSpec((INPROJ_TILE, D_MODEL), lambda r, j: (r, 0))
    tab = pl.BlockSpec((nbatch + 1, SUBLANES, D_MODEL), lambda r, j: (0, 0, 0))
    in_specs = [row] + ([row, tab] if has_resid else []) + [
        tab, pl.BlockSpec((1, D_MODEL), lambda r, j: (0, 0)),
        pl.BlockSpec((D_MODEL, IN_COL_TILE), lambda r, j: (0, j))]
    out_specs = [pl.BlockSpec((INPROJ_TILE, IN_COL_TILE), lambda r, j: (r, j))] + ([row] if has_resid else [])
    out_shape = [jax.ShapeDtypeStruct((R, IN_COLS), BF16)] + (
        [jax.ShapeDtypeStruct((R, D_MODEL), F32)] if has_resid else [])
    args = [x] + ([p, mod_prev] if has_resid else []) + [mod, norm_g.reshape(1, D_MODEL), w_in_cols]
    outs = pl.pallas_call(
        functools.partial(_inproj_kernel, has_resid=has_resid, nb=nb, nbatch=nbatch),
        grid=(R // INPROJ_TILE, IN_COLS // IN_COL_TILE),
        in_specs=in_specs, out_specs=out_specs, out_shape=out_shape,
        scratch_shapes=[pltpu.VMEM((INPROJ_TILE, D_MODEL), BF16)],
        compiler_params=_params("parallel", "arbitrary"),
        name="in_projection",
    )(*args)
    return (outs[0], outs[1]) if has_resid else (outs[0], x)


def reorder_w_in(w_in):
    parts = dict(zip(("glu", "q", "k", "v", "g", "a", "nq", "nk", "nv", "br"),
                     jnp.split(w_in, np.cumsum(REF_SPLITS)[:-1].tolist(), axis=-1)))
    a_pad = jnp.zeros((D_MODEL, LANES - 2 * GLA_RANK), w_in.dtype)
    cols = [parts[n] for n in ("br", "glu", "v", "g", "nq", "nk", "nv", "q", "k", "a")] + [a_pad]
    return jnp.concatenate(cols, axis=-1).astype(BF16)


def _conv_kernel(prev_ref, cur_ref, next_ref, w_ref, b_ref, lg_ref, lb_ref, o_ref, ext_ref, *, nb):
    n = pl.program_id(1)
    half = CONV_WIDTH // 2
    pad = 2 * SUBLANES

    def glu(x):
        x = x.astype(F32)
        return x[:, :CONV_DIM] * _sigmoid(x[:, CONV_DIM:])

    has_prev = n >= 2
    has_next = (n >= 1) & (n <= nb - 2)
    ext_ref[0:pad, :] = jnp.where(has_prev, glu(prev_ref[0, SUB - pad:SUB, :]), 0.0)
    ext_ref[pad:pad + SUB, :] = glu(cur_ref[0])
    ext_ref[pad + SUB:pad + SUB + pad, :] = jnp.where(has_next, glu(next_ref[0, 0:pad, :]), 0.0)
    acc = jnp.zeros((SUB, CONV_DIM), F32)
    for j in range(CONV_WIDTH):
        off = pad - half + j
        acc = acc + ext_ref[off:off + SUB, :] * w_ref[j:j + 1, :]
    u = acc + b_ref[...]
    mu = jnp.mean(u, axis=-1, keepdims=True)
    uc = u - mu
    var = jnp.mean(uc * uc, axis=-1, keepdims=True)
    y = uc * lax.rsqrt(var + EPS) * lg_ref[...] + lb_ref[...]
    o_ref[0] = (y * _sigmoid(y)).astype(BF16)


def conv_branch(proj, conv_w, conv_b, ln_g, ln_b, nb):
    B, L, _ = proj.shape
    cb = COL_GLU // (2 * CONV_DIM)
    blk = (1, SUB, 2 * CONV_DIM)
    vec = pl.BlockSpec((1, CONV_DIM), lambda b, n: (0, 0))
    return pl.pallas_call(
        functools.partial(_conv_kernel, nb=nb),
        grid=(B, nb),
        in_specs=[pl.BlockSpec(blk, lambda b, n: (b, jnp.maximum(n - 1, 0), cb)),
                  pl.BlockSpec(blk, lambda b, n: (b, n, cb)),
                  pl.BlockSpec(blk, lambda b, n: (b, jnp.minimum(n + 1, nb - 1), cb)),
                  pl.BlockSpec((CONV_WIDTH + 1, CONV_DIM), lambda b, n: (0, 0)), vec, vec, vec],
        out_specs=pl.BlockSpec((1, SUB, CONV_DIM), lambda b, n: (b, n, 0)),
        out_shape=jax.ShapeDtypeStruct((B, L, CONV_DIM), BF16),
        scratch_shapes=[pltpu.VMEM((SUB + 4 * SUBLANES, CONV_DIM), F32)],
        compiler_params=_params("parallel", "parallel"),
        name="conv_branch",
    )(proj, proj, proj, jnp.concatenate([conv_w, jnp.zeros((1, CONV_DIM), F32)], axis=0),
      conv_b.reshape(1, CONV_DIM), ln_g.reshape(1, CONV_DIM), ln_b.reshape(1, CONV_DIM))


def gla_rope_tables(n_ctx, n_lat):
    t = jnp.arange(n_lat)
    row = (t // GRID_W).astype(F32)
    col = (t % GRID_W).astype(F32)
    n_freq = GLA_DK // 4
    inv = ROPE_BASE ** (-jnp.arange(n_freq, dtype=F32) / n_freq)
    ang = jnp.concatenate([row[:, None] * inv, col[:, None] * inv], axis=-1)
    cos = jnp.repeat(jnp.cos(ang), 2, axis=-1)
    sin = jnp.repeat(jnp.sin(ang), 2, axis=-1)
    sign = jnp.tile(jnp.array([-1.0, 1.0], F32), GLA_DK // 2)
    cos = jnp.tile(cos, (1, GLA_HEADS))
    sin = jnp.tile(sin * sign, (1, GLA_HEADS))
    cos = jnp.concatenate([jnp.ones((n_ctx, GLA_QK), F32), cos], axis=0)
    sin = jnp.concatenate([jnp.zeros((n_ctx, GLA_QK), F32), sin], axis=0)
    return cos, sin


def _swap_pairs(x):
    lane = lax.broadcasted_iota(jnp.int32, x.shape, 1)
    return jnp.where(lane % 2 == 0, pltpu.roll(x, LANES - 1, axis=1), pltpu.roll(x, 1, axis=1))


def _rope(x, cos, sin):
    return jnp.concatenate([x[:, p * LANES:(p + 1) * LANES] * cos[:, p * LANES:(p + 1) * LANES]
                            + _swap_pairs(x[:, p * LANES:(p + 1) * LANES]) * sin[:, p * LANES:(p + 1) * LANES]
                            for p in range(GLA_QK // LANES)], axis=1)


def _split3(x):
    hi = x.astype(BF16)
    r = x - hi.astype(F32)
    mid = r.astype(BF16)
    lo = (r - mid.astype(F32)).astype(BF16)
    return hi, mid, lo


def _gla_chunk(q, k, v, a, cos, sin, wa, ba, state_ref, backward):
    C = GLA_CHUNK
    qr = _rope(q.astype(F32) * np.float32(GLA_DK ** -0.5), cos, sin)
    kr = _rope(k.astype(F32), cos, sin)
    z = jnp.dot(a.astype(BF16), wa, preferred_element_type=F32) + ba
    la = (jnp.minimum(z, 0.0) - jnp.log(1.0 + jnp.exp(-jnp.abs(z)))) * np.float32(1.0 / GLA_GATE_NORM)
    ri = lax.broadcasted_iota(jnp.int32, (C, C), 0)
    ci = lax.broadcasted_iota(jnp.int32, (C, C), 1)
    keep = (ci >= ri) if backward else (ci <= ri)
    tri = jnp.where(keep, 1.0, 0.0).astype(BF16)
    hi, mid, lo = _split3(la)
    b = (jnp.dot(tri, hi, preferred_element_type=F32) + jnp.dot(tri, mid, preferred_element_type=F32)
         + jnp.dot(tri, lo, preferred_element_type=F32))
    mid_row = C // 2 - 1 if backward else C // 2
    end_row = 0 if backward else C - 1
    b_mid = b[mid_row:mid_row + 1]
    b_end = b[end_row:end_row + 1]
    qe = qr * jnp.exp(b)
    qa = qr * jnp.exp(b - b_mid)
    ka = kr * jnp.exp(b_mid - b)
    ke = kr * jnp.exp(b_end - b)
    decay = jnp.exp(b_end)
    lane = lax.broadcasted_iota(jnp.int32, (C, LANES), 1)
    row128 = lax.broadcasted_iota(jnp.int32, (LANES, LANES), 0)
    outs = []
    for p in range(GLA_QK // LANES):
        ps = slice(p * LANES, (p + 1) * LANES)
        s_prev = state_ref[p]
        s_b = s_prev.astype(BF16)
        ka_p = ka[:, ps].astype(BF16)
        ke_t = ke[:, ps].T.astype(BF16)
        upd = []
        for hh in range(2):
            h = 2 * p + hh
            in_head = (lane >= hh * GLA_DK) & (lane < (hh + 1) * GLA_DK)
            v_h = v[:, h * GLA_DV:(h + 1) * GLA_DV].astype(BF16)
            att = lax.dot_general(jnp.where(in_head, qa[:, ps], 0.0).astype(BF16), ka_p, NT_DIMS,
                                  preferred_element_type=F32)
            att = jnp.where(keep, att, 0.0)
            o_h = (jnp.dot(att.astype(BF16), v_h, preferred_element_type=F32)
                   + jnp.dot(jnp.where(in_head, qe[:, ps], 0.0).astype(BF16), s_b, preferred_element_type=F32))
            outs.append(o_h)
            upd.append(jnp.dot(ke_t, v_h, preferred_element_type=F32))
        d_col = jnp.broadcast_to(decay[:, ps], (LANES, LANES)).T
        state_ref[p] = s_prev * d_col + jnp.where(row128 < GLA_DK, upd[0], upd[1])
    return jnp.concatenate(outs, axis=1)


def _gla_kernel(qf_ref, kf_ref, vf_ref, af_ref, cosf_ref, sinf_ref,
                qb_ref, kb_ref, vb_ref, ab_ref, cosb_ref, sinb_ref,
                wa_ref, ba_ref, of_ref, ob_ref, sf_ref, sb_ref):
    n = pl.program_id(1)

    @pl.when(n == 0)
    def _():
        sf_ref[...] = jnp.zeros(sf_ref.shape, F32)
        sb_ref[...] = jnp.zeros(sb_ref.shape, F32)

    nch = SUB // GLA_CHUNK
    for j in range(nch):
        rf = slice(j * GLA_CHUNK, (j + 1) * GLA_CHUNK)
        of_ref[0, rf, :] = _gla_chunk(qf_ref[0, rf, :], kf_ref[0, rf, :], vf_ref[0, rf, :], af_ref[0, rf, :],
                                      cosf_ref[rf, :], sinf_ref[rf, :], wa_ref[0], ba_ref[0:1, :], sf_ref, False)
        jb = nch - 1 - j
        rb = slice(jb * GLA_CHUNK, (jb + 1) * GLA_CHUNK)
        ob_ref[0, rb, :] = _gla_chunk(qb_ref[0, rb, :], kb_ref[0, rb, :], vb_ref[0, rb, :], ab_ref[0, rb, :],
                                      cosb_ref[rb, :], sinb_ref[rb, :], wa_ref[1], ba_ref[1:2, :], sb_ref, True)


def gla_branch(proj, wa2, ba, nb):
    B, L, _ = proj.shape
    cos, sin = gla_rope_tables(SUB, L - SUB)
    wa = jnp.zeros((2, LANES, GLA_QK), F32)
    wa = wa.at[0, :GLA_RANK].set(wa2[0]).at[1, GLA_RANK:2 * GLA_RANK].set(wa2[1]).astype(BF16)

    def fwd(b, n):
        return (b, n)

    def bwd(b, n):
        return (b, jnp.where(n == 0, 0, nb - n))

    def specs(f):
        def col(width, offset):
            return pl.BlockSpec((1, SUB, width), lambda b, n: f(b, n) + (offset // width,))
        tab = pl.BlockSpec((SUB, GLA_QK), lambda b, n: (f(b, n)[1], 0))
        return [col(GLA_QK, COL_Q), col(GLA_QK, COL_K), col(GLA_V, COL_V), col(LANES, COL_A), tab, tab]

    return pl.pallas_call(
        _gla_kernel,
        grid=(B, nb),
        in_specs=specs(fwd) + specs(bwd) + [pl.BlockSpec((2, LANES, GLA_QK), lambda b, n: (0, 0, 0)),
                                            pl.BlockSpec((2, GLA_QK), lambda b, n: (0, 0))],
        out_specs=[pl.BlockSpec((1, SUB, GLA_V), lambda b, n: fwd(b, n) + (0,)),
                   pl.BlockSpec((1, SUB, GLA_V), lambda b, n: bwd(b, n) + (0,))],
        out_shape=[jax.ShapeDtypeStruct((B, L, GLA_V), F32), jax.ShapeDtypeStruct((B, L, GLA_V), F32)],
        scratch_shapes=[pltpu.VMEM((GLA_QK // LANES, LANES, GLA_DV), F32),
                        pltpu.VMEM((GLA_QK // LANES, LANES, GLA_DV), F32)],
        compiler_params=_params("parallel", "arbitrary"),
        name="gla_scan",
    )(proj, proj, proj, proj, cos, sin, proj, proj, proj, proj, cos, sin, wa, ba)


def na_key_row_start(x, rows):
    return min(max(NA_QROWS * x - NA_KH // 2, 0), rows - NA_KROWS)


def na_bias_blocks(rpb, rows):
    W = GRID_W
    rs = np.clip(np.arange(rows) - NA_KH // 2, 0, rows - NA_KH)
    cs = np.clip(np.arange(W) - NA_KW // 2, 0, W - NA_KW)
    nrb = rows // NA_QROWS
    blocks = (0, 1, nrb - 1)
    qr = np.stack([np.arange(NA_QROWS) + NA_QROWS * x for x in blocks])
    kr = np.stack([na_key_row_start(x, rows) + np.arange(NA_KROWS) for x in blocks])
    ok_r = (kr[:, None, :] >= rs[qr][:, :, None]) & (kr[:, None, :] < rs[qr][:, :, None] + NA_KH)
    i_r = np.clip(kr[:, None, :] - qr[:, :, None] + NA_KH - 1, 0, 2 * NA_KH - 2)
    qc = np.arange(W)
    kc = np.arange(W)
    ok_c = (kc[None, :] >= cs[qc][:, None]) & (kc[None, :] < cs[qc][:, None] + NA_KW)
    i_c = np.clip(kc[None, :] - qc[:, None] + NA_KW - 1, 0, 2 * NA_KW - 2)
    ok = ok_r[:, :, None, :, None] & ok_c[None, None, :, None, :]
    pick_r = (i_r[..., None] == np.arange(2 * NA_KH - 1)).astype(np.float32)
    pick_c = (i_c[..., None] == np.arange(2 * NA_KW - 1)).astype(np.float32)
    rows_sel = jnp.einsum('tajr,hrc->htajc', pick_r, rpb.astype(F32), precision=lax.Precision.HIGHEST)
    vals = jnp.einsum('htajc,qkc->htaqjk', rows_sel, pick_c, precision=lax.Precision.HIGHEST)
    bias = jnp.where(ok[None], vals, -jnp.inf)
    return bias.reshape(rpb.shape[0], 3, NA_QROWS * W, NA_KROWS * W)


def _na_kernel(q_ref, k_ref, v_ref, bias_ref, o_ref, *, rows):
    n = pl.program_id(2)
    W = GRID_W
    q = q_ref[0].astype(F32) * np.float32(NA_HD ** -0.5)
    kc = k_ref[0, 0:SUB, :].astype(BF16)
    vc = v_ref[0, 0:SUB, :].astype(BF16)
    lane = lax.broadcasted_iota(jnp.int32, (SUB, LANES), 1)

    def head_q(hh):
        in_head = (lane >= hh * NA_HD) & (lane < (hh + 1) * NA_HD)
        return jnp.where(in_head, q, 0.0).astype(BF16)

    @pl.when(n == 0)
    def _():
        outs = []
        for hh in range(2):
            sc = lax.dot_general(head_q(hh), kc, NT_DIMS, preferred_element_type=F32)
            pc = jnp.exp(sc - jnp.max(sc, axis=-1, keepdims=True))
            o = jnp.dot(pc.astype(BF16), vc, preferred_element_type=F32)
            outs.append(o / jnp.sum(pc, axis=-1, keepdims=True))
        o_ref[0] = jnp.where(lane < NA_HD, outs[0], outs[1]).astype(BF16)

    @pl.when(n > 0)
    def _():
        start_row = jnp.minimum(jnp.maximum(NA_QROWS * (n - 1) - NA_KH // 2, 0), rows - NA_KROWS)
        win = pl.ds(pl.multiple_of(SUB + start_row * W, W), NA_KROWS * W)
        kw = k_ref[0, win, :].astype(BF16)
        vw = v_ref[0, win, :].astype(BF16)
        outs = []
        for hh in range(2):
            qm = head_q(hh)
            s = lax.dot_general(qm, kw, NT_DIMS, preferred_element_type=F32) + bias_ref[hh, 0]
            sc = lax.dot_general(qm, kc, NT_DIMS, preferred_element_type=F32)
            m = jnp.maximum(jnp.max(s, axis=-1, keepdims=True), jnp.max(sc, axis=-1, keepdims=True))
            p = jnp.exp(s - m)
            pc = jnp.exp(sc - m)
            l = jnp.sum(p, axis=-1, keepdims=True) + jnp.sum(pc, axis=-1, keepdims=True)
            o = (jnp.dot(p.astype(BF16), vw, preferred_element_type=F32)
                 + jnp.dot(pc.astype(BF16), vc, preferred_element_type=F32))
            outs.append(o / l)
        o_ref[0] = jnp.where(lane < NA_HD, outs[0], outs[1]).astype(BF16)


def na_branch(proj, rpb, nb):
    B, L, _ = proj.shape
    rows = (L - SUB) // GRID_W
    nrb = nb - 1
    bias = na_bias_blocks(rpb, rows)

    def bias_idx(p, b, n):
        return (p, jnp.where(n <= 1, 0, jnp.where(n == nrb, 2, 1)), 0, 0)

    return pl.pallas_call(
        functools.partial(_na_kernel, rows=rows),
        grid=(NA_DIM // LANES, B, nb),
        in_specs=[pl.BlockSpec((1, SUB, LANES), lambda p, b, n: (b, n, COL_NQ // LANES + p)),
                  pl.BlockSpec((1, L, LANES), lambda p, b, n: (b, 0, COL_NK // LANES + p)),
                  pl.BlockSpec((1, L, LANES), lambda p, b, n: (b, 0, COL_NV // LANES + p)),
                  pl.BlockSpec((2, 1, NA_QROWS * GRID_W, NA_KROWS * GRID_W), bias_idx)],
        out_specs=pl.BlockSpec((1, SUB, LANES), lambda p, b, n: (b, n, p)),
        out_shape=jax.ShapeDtypeStruct((B, L, NA_DIM), BF16),
        compiler_params=_params("parallel", "parallel", "arbitrary"),
        name="na_attention",
    )(proj, proj, proj, bias)


def _merge_kernel(conv_ref, of_ref, ob_ref, pg_ref, na_ref, br_ref, x_ref, mod_ref, wc_ref, wg_ref, wn_ref,
                  wo_ref, bg_ref, gn_ref, n2_ref, xo_ref, h2_ref, *, nb, nbatch):
    r = pl.program_id(0)
    o = of_ref[...] + ob_ref[...]
    on = jnp.concatenate([_rms(o[:, h * GLA_DV:(h + 1) * GLA_DV]) for h in range(GLA_HEADS)], axis=1) * gn_ref[...]
    pg = pg_ref[...].astype(F32)
    y_gla = jnp.dot((on * (pg * _sigmoid(pg))).astype(BF16), wg_ref[...], preferred_element_type=F32)
    y_conv = jnp.dot(conv_ref[...], wc_ref[...], preferred_element_type=F32)
    y_na = jnp.dot(na_ref[...], wn_ref[...], preferred_element_type=F32)
    def gate(k):
        cols = slice(k * D_MODEL, (k + 1) * D_MODEL)
        return _sigmoid(br_ref[:, cols].astype(F32) + bg_ref[:, cols])

    m = gate(0) * y_conv + gate(1) * y_gla + gate(2) * y_na
    y = jnp.dot(m.astype(BF16), wo_ref[...], preferred_element_type=F32)
    for s in range(ROW_TILE // SUB):
        rows = slice(s * SUB, (s + 1) * SUB)
        md = mod_ref[_mod_index(r * (ROW_TILE // SUB) + s, nb, nbatch)]
        x = x_ref[rows, :] + md[2:3, :] * y[rows, :]
        xo_ref[rows, :] = x
        h2_ref[rows, :] = (_rms(x) * n2_ref[...] * (1.0 + md[4:5, :]) + md[3:4, :]).astype(BF16)


def merge_branches(conv_act, o_f, o_b, y_na, proj, x, mod, w_conv_o, w_gla_o, w_na_o, w_out, b_gate,
                   gla_norm_g, norm2_g, nb, nbatch):
    R = x.shape[0]

    def rows(width, block=0):
        return pl.BlockSpec((ROW_TILE, width), lambda r: (r, block))

    def whole(shape):
        return pl.BlockSpec(shape, lambda r: (0,) * len(shape))

    return pl.pallas_call(
        functools.partial(_merge_kernel, nb=nb, nbatch=nbatch),
        grid=(R // ROW_TILE,),
        in_specs=[rows(CONV_DIM), rows(GLA_V), rows(GLA_V), rows(GLA_V, COL_G // GLA_V), rows(NA_DIM),
                  rows(N_BRANCH * D_MODEL, COL_BR // (N_BRANCH * D_MODEL)), rows(D_MODEL),
                  whole((nbatch + 1, SUBLANES, D_MODEL)),
                  whole((CONV_DIM, D_MODEL)), whole((GLA_V, D_MODEL)), whole((NA_DIM, D_MODEL)),
                  whole((D_MODEL, D_MODEL)), whole((1, N_BRANCH * D_MODEL)), whole((1, GLA_V)),
                  whole((1, D_MODEL))],
        out_specs=[rows(D_MODEL), rows(D_MODEL)],
        out_shape=[jax.ShapeDtypeStruct((R, D_MODEL), F32), jax.ShapeDtypeStruct((R, D_MODEL), BF16)],
        compiler_params=_params("parallel"),
        name="merge_branches",
    )(conv_act, o_f, o_b, proj, y_na, proj, x, mod, w_conv_o.astype(BF16), w_gla_o.astype(BF16),
      w_na_o.astype(BF16), w_out.astype(BF16), b_gate.reshape(1, -1), gla_norm_g.reshape(1, -1),
      norm2_g.reshape(1, -1))


def _peer_route_kernel(h_ref, wq_ref, keys_ref, st_ref, stats_ref, *, nh, topk):
    tt = h_ref.shape[0]
    qry = jnp.dot(h_ref[...], wq_ref[...], preferred_element_type=F32)
    neg = jnp.float32(-jnp.inf)
    for h in range(nh):
        tops = []
        for z in range(2):
            hz = 2 * h + z
            q = qry[:, hz * LANES:(hz + 1) * LANES].astype(BF16)
            s = lax.dot_general(keys_ref[hz], q, NT_DIMS, preferred_element_type=F32)
            st_ref[hz] = s
            cur = s
            vals = []
            for _ in range(topk):
                m = jnp.max(cur, axis=0, keepdims=True)
                vals.append(m)
                cur = jnp.where(cur == m, neg, cur)
            tops.append(vals)
        a, b = tops
        cands = [a[p] + b[q] for p in range(topk) for q in range(topk) if (p + 1) * (q + 1) <= topk]
        npad = (-len(cands)) % SUBLANES
        cand = jnp.concatenate(cands + [jnp.full((npad, tt), neg, F32)], axis=0)
        cur = cand
        tau = None
        for _ in range(topk):
            tau = jnp.max(cur, axis=0, keepdims=True)
            cur = jnp.where(cur == tau, neg, cur)
        top = a[0] + b[0]
        zsum = jnp.sum(jnp.where(cand >= tau, jnp.exp(cand - top), 0.0), axis=0, keepdims=True)
        stats_ref[h] = jnp.concatenate([tau, a[0], 1.0 / zsum, jnp.zeros((SUBLANES - 3, tt), F32)] + b, axis=0)


def _peer_main_kernel(hb_ref, st_ref, stats_ref, u_ref, vt_ref, out_ref, acc_ref, e0_ref, cnt_ref, e1_ref, rank_ref,
                      *, nh, eb, tt, topk):
    i = pl.program_id(1)

    @pl.when(i == 0)
    def _():
        acc_ref[...] = jnp.zeros(acc_ref.shape, F32)
        for h in range(nh):
            tau = stats_ref[h, 0:1, :]
            a1 = stats_ref[h, 1:2, :]
            rz = stats_ref[h, 2:3, :]
            s0 = st_ref[2 * h]
            s1 = st_ref[2 * h + 1]
            e0_ref[h] = jnp.exp(s0 - a1) * rz
            e1_ref[h] = jnp.exp(s1 - stats_ref[h, SUBLANES:SUBLANES + 1, :]).astype(BF16)
            cnt = jnp.zeros(s0.shape, F32)
            rank = jnp.zeros(s1.shape, F32)
            for q in range(topk):
                bq = stats_ref[h, SUBLANES + q:SUBLANES + q + 1, :]
                cnt = cnt + jnp.where(s0 + bq >= tau, 1.0, 0.0)
                rank = rank + jnp.where(bq > s1, 1.0, 0.0)
            cnt_ref[h] = cnt
            rank_ref[h] = rank.astype(BF16)

    nsub = eb // LANES
    assert nsub == SUBLANES
    first = pl.ds(pl.multiple_of(i * SUBLANES, SUBLANES), SUBLANES)
    nchunk = tt // PEER_CHUNK

    def act_of(c):
        return lax.dot_general(u_ref[...], hb_ref[c * PEER_CHUNK:(c + 1) * PEER_CHUNK, :], NT_DIMS,
                               preferred_element_type=F32)

    act_next = act_of(0)
    for c in range(nchunk):
        cs = slice(c * PEER_CHUNK, (c + 1) * PEER_CHUNK)
        act = act_next
        if c + 1 < nchunk:
            act_next = act_of(c + 1)
        e0 = [e0_ref[h, first, cs] for h in range(nh)]
        cnt = [cnt_ref[h, first, cs] for h in range(nh)]
        ws = []
        for s in range(nsub):
            g = jnp.zeros((LANES, PEER_CHUNK), BF16)
            for h in range(nh):
                keep = rank_ref[h, :, cs] < cnt[h][s:s + 1].astype(BF16)
                g = g + jnp.where(keep, e1_ref[h, :, cs] * e0[h][s:s + 1].astype(BF16), jnp.zeros((), BF16))
            a = act[s * LANES:(s + 1) * LANES]
            ge = 0.5 * a * (1.0 + lax.erf(a * np.float32(math.sqrt(0.5))))
            ws.append(g * ge.astype(BF16))
        w = jnp.concatenate(ws, axis=0)
        acc_ref[:, cs] += jnp.dot(vt_ref[...], w, preferred_element_type=F32)

    @pl.when(i == pl.num_programs(1) - 1)
    def _():
        out_ref[...] = acc_ref[...].T


def peer_ffn(h2, wq, keys, u_tab, v_tab):
    T, D = h2.shape
    tt, eb, topk = PEER_TT, PEER_EB, PEER_TOPK
    nh = keys.shape[0]
    nk = keys.shape[2]
    assert nk == LANES and keys.shape[3] == LANES and T % tt == 0
    E = nk * nk
    wq_b = wq.astype(BF16)
    keys_b = keys.reshape(2 * nh, nk, LANES).astype(BF16)
    nt = T // tt
    st, stats = pl.pallas_call(
        functools.partial(_peer_route_kernel, nh=nh, topk=topk),
        grid=(nt,),
        in_specs=[pl.BlockSpec((tt, D), lambda t: (t, 0)),
                  pl.BlockSpec((D, 2 * nh * LANES), lambda t: (0, 0)),
                  pl.BlockSpec((2 * nh, nk, LANES), lambda t: (0, 0, 0))],
        out_specs=[pl.BlockSpec((2 * nh, nk, tt), lambda t: (0, 0, t)),
                   pl.BlockSpec((nh, SUBLANES + topk, tt), lambda t: (0, 0, t))],
        out_shape=[jax.ShapeDtypeStruct((2 * nh, nk, T), F32),
                   jax.ShapeDtypeStruct((nh, SUBLANES + topk, T), F32)],
        compiler_params=_params("parallel"),
        name="peer_route",
    )(h2, wq_b, keys_b)
    u_b = u_tab.astype(BF16)
    vt_b = v_tab.T.astype(BF16)
    return pl.pallas_call(
        functools.partial(_peer_main_kernel, nh=nh, eb=eb, tt=tt, topk=topk),
        grid=(nt, E // eb),
        in_specs=[pl.BlockSpec((tt, D), lambda t, i: (t, 0), pipeline_mode=pl.Buffered(1)),
                  pl.BlockSpec((2 * nh, nk, tt), lambda t, i: (0, 0, t), pipeline_mode=pl.Buffered(1)),
                  pl.BlockSpec((nh, SUBLANES + topk, tt), lambda t, i: (0, 0, t)),
                  pl.BlockSpec((eb, D), lambda t, i: (i, 0)),
                  pl.BlockSpec((D, eb), lambda t, i: (0, i))],
        out_specs=pl.BlockSpec((tt, D), lambda t, i: (t, 0)),
        out_shape=jax.ShapeDtypeStruct((T, D), F32),
        scratch_shapes=[pltpu.VMEM((D, tt), F32),
                        pltpu.VMEM((nh, nk, tt), F32),
                        pltpu.VMEM((nh, nk, tt), F32),
                        pltpu.VMEM((nh, nk, tt), BF16),
                        pltpu.VMEM((nh, nk, tt), BF16)],
        compiler_params=_params("parallel", "arbitrary"),
        name="peer_main",
    )(h2, st, stats, u_b, vt_b)


def _final_kernel(x_ref, p_ref, mod_ref, g_ref, o_ref):
    b = pl.program_id(0)
    x = x_ref[0] + mod_ref[b][5:6, :] * p_ref[0]
    o_ref[0] = _rms(x) * g_ref[...]


def final_norm(x, p, mod, g, nbatch):
    B, L, D = x.shape
    blk = pl.BlockSpec((1, SUB, D), lambda b, n: (b, n + 1, 0))
    return pl.pallas_call(
        _final_kernel,
        grid=(B, L // SUB - 1),
        in_specs=[blk, blk, pl.BlockSpec((nbatch + 1, SUBLANES, D), lambda b, n: (0, 0, 0)),
                  pl.BlockSpec((1, D), lambda b, n: (0, 0))],
        out_specs=pl.BlockSpec((1, SUB, D), lambda b, n: (b, n, 0)),
        out_shape=jax.ShapeDtypeStruct((B, L - SUB, D), F32),
        compiler_params=_params("parallel", "parallel"),
        name="final_norm",
    )(x, p, mod, g.reshape(1, D))


def kernel(x, c, ctx, c_ctx, w_mod, b_mod, norm1_g, norm2_g, w_in, b_gate, conv_w, conv_b,
           conv_ln_g, conv_ln_b, w_conv_o, gla_wa2, gla_ba, gla_norm_g, w_gla_o, na_rpb, w_na_o,
           w_out, peer_wq, peer_keys, peer_u, peer_v, final_g):
    B, S, D = x.shape
    L = ctx.shape[1] + S
    assert ctx.shape[1] == SUB and S % SUB == 0 and (B * L) % PEER_TT == 0 and (B * L) % INPROJ_TILE == 0
    nb = L // SUB
    R = B * L
    xs = jnp.concatenate([ctx, x], axis=1).reshape(R, D)
    p = None
    mod_prev = None
    for i in range(DEPTH):
        mod = mod_table(c, c_ctx, w_mod[i], b_mod[i])
        proj, xs = in_projection(xs, p, mod_prev, mod, norm1_g[i], reorder_w_in(w_in[i]), nb, B)
        proj3 = proj.reshape(B, L, IN_COLS)
        conv_act = conv_branch(proj3, conv_w[i], conv_b[i], conv_ln_g[i], conv_ln_b[i], nb)
        o_f, o_b = gla_branch(proj3, gla_wa2[i], gla_ba[i], nb)
        y_na = na_branch(proj3, na_rpb[i], nb)
        xs, h2 = merge_branches(conv_act.reshape(R, CONV_DIM), o_f.reshape(R, GLA_V), o_b.reshape(R, GLA_V),
                                y_na.reshape(R, NA_DIM), proj, xs, mod, w_conv_o[i], w_gla_o[i], w_na_o[i],
                                w_out[i], b_gate[i], gla_norm_g[i], norm2_g[i], nb, B)
        p = peer_ffn(h2, peer_wq[i], peer_keys[i], peer_u[i], peer_v[i])
        mod_prev = mod
    return final_norm(xs.reshape(B, L, D), p.reshape(B, L, D), mod_prev, final_g, B)
```

```python
import functools
import math
import jax
import jax.numpy as jnp
from jax import lax
import numpy as np
from jax.experimental import pallas as pl
from jax.experimental.pallas import tpu as pltpu

D_MODEL = 1024
BATCH = 8
SEQ = 4096
DEPTH = 2
GRID_W = 64
CTX_LEN = 256
EPS = 1e-6
N_BRANCH = 3
F32 = jnp.float32
BF16 = jnp.bfloat16
CONV_DIM = 512
CONV_WIDTH = 31
GLA_HEADS = 4
GLA_DK = 64
GLA_DV = 128
GLA_RANK = 16
GLA_GATE_NORM = 16.0
GLA_CHUNK = 64
GLA_QK = GLA_HEADS * GLA_DK
GLA_V = GLA_HEADS * GLA_DV
ROPE_BASE = 10000.0
NA_HEADS = 8
NA_HD = 64
NA_DIM = NA_HEADS * NA_HD
NA_KH = 8
NA_KW = 16
PEER_HEADS = 8
PEER_NKEYS = 128
PEER_TOPK = 16
PEER_DKEY = 256

LANES = 128
SUBLANES = 8
NT_DIMS = (((1,), (1,)), ((), ()))
VMEM_LIMIT = 56 * 1024 * 1024

SUB = CTX_LEN
ROW_TILE = 512
INPROJ_TILE = 1024
PEER_TT = 1024
PEER_EB = 2048
PEER_CHUNK = 256
NA_QROWS = 4
NA_KROWS = 12

COL_BR = 0
COL_GLU = COL_BR + N_BRANCH * D_MODEL
COL_V = COL_GLU + 2 * CONV_DIM
COL_G = COL_V + GLA_V
COL_NQ = COL_G + GLA_V
COL_NK = COL_NQ + NA_DIM
COL_NV = COL_NK + NA_DIM
COL_Q = COL_NV + NA_DIM
COL_K = COL_Q + GLA_QK
COL_A = COL_K + GLA_QK
IN_COLS = COL_A + LANES
IN_COL_TILE = IN_COLS // 3
REF_SPLITS = (2 * CONV_DIM, GLA_QK, GLA_QK, GLA_V, GLA_V, 2 * GLA_RANK, NA_DIM, NA_DIM, NA_DIM, N_BRANCH * D_MODEL)


def _params(*sem):
    return pltpu.CompilerParams(dimension_semantics=sem, vmem_limit_bytes=VMEM_LIMIT)


def _mod_index(block, nb, nbatch):
    return jnp.where(block % nb == 0, nbatch, block // nb)


def _sigmoid(x):
    return 1.0 / (1.0 + jnp.exp(-x))


def _rms(x):
    return x * lax.rsqrt(jnp.mean(x * x, axis=-1, keepdims=True) + EPS)


def _mod_kernel(c_ref, w_ref, b_ref, o_ref):
    c = c_ref[...]
    cond = (c * _sigmoid(c)).astype(BF16)
    o_ref[...] = jnp.dot(cond, w_ref[...].astype(BF16), preferred_element_type=F32) + b_ref[...]


def mod_table(c, c_ctx, w_mod, b_mod):
    nb = c.shape[0]
    cond = jnp.concatenate([c, c_ctx[None], jnp.zeros((2 * SUBLANES - nb - 1, D_MODEL), F32)], axis=0)
    n = 6 * D_MODEL
    tn = n // 4
    out = pl.pallas_call(
        _mod_kernel,
        grid=(n // tn,),
        in_specs=[pl.BlockSpec((2 * SUBLANES, D_MODEL), lambda j: (0, 0)),
                  pl.BlockSpec((D_MODEL, tn), lambda j: (0, j)),
                  pl.BlockSpec((1, tn), lambda j: (0, j))],
        out_specs=pl.BlockSpec((2 * SUBLANES, tn), lambda j: (0, j)),
        out_shape=jax.ShapeDtypeStruct((2 * SUBLANES, n), F32),
        compiler_params=_params("parallel"),
        name="mod_table",
    )(cond, w_mod, b_mod.reshape(1, n))
    tab = out[:nb + 1].reshape(nb + 1, 6, D_MODEL)
    return jnp.concatenate([tab, jnp.zeros((nb + 1, 2, D_MODEL), F32)], axis=1)


def _inproj_kernel(*refs, has_resid, nb, nbatch):
    if has_resid:
        x_ref, p_ref, modp_ref, mod_ref, g_ref, w_ref, o_ref, xo_ref, h_ref = refs
    else:
        x_ref, mod_ref, g_ref, w_ref, o_ref, h_ref = refs
    r = pl.program_id(0)

    @pl.when(pl.program_id(1) == 0)
    def _():
        for s in range(INPROJ_TILE // SUB):
            rows = slice(s * SUB, (s + 1) * SUB)
            idx = _mod_index(r * (INPROJ_TILE // SUB) + s, nb, nbatch)
            x = x_ref[rows, :]
            if has_resid:
                x = x + modp_ref[idx][5:6, :] * p_ref[rows, :]
                xo_ref[rows, :] = x
            m = mod_ref[idx]
            h_ref[rows, :] = (_rms(x) * g_ref[...] * (1.0 + m[1:2, :]) + m[0:1, :]).astype(BF16)

    o_ref[...] = jnp.dot(h_ref[...], w_ref[...], preferred_element_type=F32).astype(BF16)


def in_projection(x, p, mod_prev, mod, norm_g, w_in_cols, nb, nbatch):
    R = x.shape[0]
    has_resid = p is not None
    row = pl.BlockSpec((INPROJ_TILE, D_MODEL), lambda r, j: (r, 0))
    tab = pl.BlockSpec((nbatch + 1, SUBLANES, D_MODEL), lambda r, j: (0, 0, 0))
    in_specs = [row] + ([row, tab] if has_resid else []) + [
        tab, pl.BlockSpec((1, D_MODEL), lambda r, j: (0, 0)),
        pl.BlockSpec((D_MODEL, IN_COL_TILE), lambda r, j: (0, j))]
    out_specs = [pl.BlockSpec((INPROJ_TILE, IN_COL_TILE), lambda r, j: (r, j))] + ([row] if has_resid else [])
    out_shape = [jax.ShapeDtypeStruct((R, IN_COLS), BF16)] + (
        [jax.ShapeDtypeStruct((R, D_MODEL), F32)] if has_resid else [])
    args = [x] + ([p, mod_prev] if has_resid else []) + [mod, norm_g.reshape(1, D_MODEL), w_in_cols]
    outs = pl.pallas_call(
        functools.partial(_inproj_kernel, has_resid=has_resid, nb=nb, nbatch=nbatch),
        grid=(R // INPROJ_TILE, IN_COLS // IN_COL_TILE),
        in_specs=in_specs, out_specs=out_specs, out_shape=out_shape,
        scratch_shapes=[pltpu.VMEM((INPROJ_TILE, D_MODEL), BF16)],
        compiler_params=_params("parallel", "arbitrary"),
        name="in_projection",
    )(*args)
    return (outs[0], outs[1]) if has_resid else (outs[0], x)


def reorder_w_in(w_in):
    parts = dict(zip(("glu", "q", "k", "v", "g", "a", "nq", "nk", "nv", "br"),
                     jnp.split(w_in, np.cumsum(REF_SPLITS)[:-1].tolist(), axis=-1)))
    a_pad = jnp.zeros((D_MODEL, LANES - 2 * GLA_RANK), w_in.dtype)
    cols = [parts[n] for n in ("br", "glu", "v", "g", "nq", "nk", "nv", "q", "k", "a")] + [a_pad]
    return jnp.concatenate(cols, axis=-1).astype(BF16)


def _conv_kernel(prev_ref, cur_ref, next_ref, w_ref, b_ref, lg_ref, lb_ref, o_ref, ext_ref, *, nb):
    n = pl.program_id(1)
    half = CONV_WIDTH // 2
    pad = 2 * SUBLANES

    def glu(x):
        x = x.astype(F32)
        return x[:, :CONV_DIM] * _sigmoid(x[:, CONV_DIM:])

    has_prev = n >= 2
    has_next = (n >= 1) & (n <= nb - 2)
    ext_ref[0:pad, :] = jnp.where(has_prev, glu(prev_ref[0, SUB - pad:SUB, :]), 0.0)
    ext_ref[pad:pad + SUB, :] = glu(cur_ref[0])
    ext_ref[pad + SUB:pad + SUB + pad, :] = jnp.where(has_next, glu(next_ref[0, 0:pad, :]), 0.0)
    acc = jnp.zeros((SUB, CONV_DIM), F32)
    for j in range(CONV_WIDTH):
        off = pad - half + j
        acc = acc + ext_ref[off:off + SUB, :] * w_ref[j:j + 1, :]
    u = acc + b_ref[...]
    mu = jnp.mean(u, axis=-1, keepdims=True)
    uc = u - mu
    var = jnp.mean(uc * uc, axis=-1, keepdims=True)
    y = uc * lax.rsqrt(var + EPS) * lg_ref[...] + lb_ref[...]
    o_ref[0] = (y * _sigmoid(y)).astype(BF16)


def conv_branch(proj, conv_w, conv_b, ln_g, ln_b, nb):
    B, L, _ = proj.shape
    cb = COL_GLU // (2 * CONV_DIM)
    blk = (1, SUB, 2 * CONV_DIM)
    vec = pl.BlockSpec((1, CONV_DIM), lambda b, n: (0, 0))
    return pl.pallas_call(
        functools.partial(_conv_kernel, nb=nb),
        grid=(B, nb),
        in_specs=[pl.BlockSpec(blk, lambda b, n: (b, jnp.maximum(n - 1, 0), cb)),
                  pl.BlockSpec(blk, lambda b, n: (b, n, cb)),
                  pl.BlockSpec(blk, lambda b, n: (b, jnp.minimum(n + 1, nb - 1), cb)),
                  pl.BlockSpec((CONV_WIDTH + 1, CONV_DIM), lambda b, n: (0, 0)), vec, vec, vec],
        out_specs=pl.BlockSpec((1, SUB, CONV_DIM), lambda b, n: (b, n, 0)),
        out_shape=jax.ShapeDtypeStruct((B, L, CONV_DIM), BF16),
        scratch_shapes=[pltpu.VMEM((SUB + 4 * SUBLANES, CONV_DIM), F32)],
        compiler_params=_params("parallel", "parallel"),
        name="conv_branch",
    )(proj, proj, proj, jnp.concatenate([conv_w, jnp.zeros((1, CONV_DIM), F32)], axis=0),
      conv_b.reshape(1, CONV_DIM), ln_g.reshape(1, CONV_DIM), ln_b.reshape(1, CONV_DIM))


def gla_rope_tables(n_ctx, n_lat):
    t = jnp.arange(n_lat)
    row = (t // GRID_W).astype(F32)
    col = (t % GRID_W).astype(F32)
    n_freq = GLA_DK // 4
    inv = ROPE_BASE ** (-jnp.arange(n_freq, dtype=F32) / n_freq)
    ang = jnp.concatenate([row[:, None] * inv, col[:, None] * inv], axis=-1)
    cos = jnp.repeat(jnp.cos(ang), 2, axis=-1)
    sin = jnp.repeat(jnp.sin(ang), 2, axis=-1)
    sign = jnp.tile(jnp.array([-1.0, 1.0], F32), GLA_DK // 2)
    cos = jnp.tile(cos, (1, GLA_HEADS))
    sin = jnp.tile(sin * sign, (1, GLA_HEADS))
    cos = jnp.concatenate([jnp.ones((n_ctx, GLA_QK), F32), cos], axis=0)
    sin = jnp.concatenate([jnp.zeros((n_ctx, GLA_QK), F32), sin], axis=0)
    return cos, sin


def _swap_pairs(x):
    lane = lax.broadcasted_iota(jnp.int32, x.shape, 1)
    return jnp.where(lane % 2 == 0, pltpu.roll(x, LANES - 1, axis=1), pltpu.roll(x, 1, axis=1))


def _rope(x, cos, sin):
    return jnp.concatenate([x[:, p * LANES:(p + 1) * LANES] * cos[:, p * LANES:(p + 1) * LANES]
                            + _swap_pairs(x[:, p * LANES:(p + 1) * LANES]) * sin[:, p * LANES:(p + 1) * LANES]
                            for p in range(GLA_QK // LANES)], axis=1)


def _split3(x):
    hi = x.astype(BF16)
    r = x - hi.astype(F32)
    mid = r.astype(BF16)
    lo = (r - mid.astype(F32)).astype(BF16)
    return hi, mid, lo


def _gla_chunk(q, k, v, a, cos, sin, wa, ba, state_ref, backward):
    C = GLA_CHUNK
    qr = _rope(q.astype(F32) * np.float32(GLA_DK ** -0.5), cos, sin)
    kr = _rope(k.astype(F32), cos, sin)
    z = jnp.dot(a.astype(BF16), wa, preferred_element_type=F32) + ba
    la = (jnp.minimum(z, 0.0) - jnp.log(1.0 + jnp.exp(-jnp.abs(z)))) * np.float32(1.0 / GLA_GATE_NORM)
    ri = lax.broadcasted_iota(jnp.int32, (C, C), 0)
    ci = lax.broadcasted_iota(jnp.int32, (C, C), 1)
    keep = (ci >= ri) if backward else (ci <= ri)
    tri = jnp.where(keep, 1.0, 0.0).astype(BF16)
    hi, mid, lo = _split3(la)
    b = (jnp.dot(tri, hi, preferred_element_type=F32) + jnp.dot(tri, mid, preferred_element_type=F32)
         + jnp.dot(tri, lo, preferred_element_type=F32))
    mid_row = C // 2 - 1 if backward else C // 2
    end_row = 0 if backward else C - 1
    b_mid = b[mid_row:mid_row + 1]
    b_end = b[end_row:end_row + 1]
    qe = qr * jnp.exp(b)
    qa = qr * jnp.exp(b - b_mid)
    ka = kr * jnp.exp(b_mid - b)
    ke = kr * jnp.exp(b_end - b)
    decay = jnp.exp(b_end)
    lane = lax.broadcasted_iota(jnp.int32, (C, LANES), 1)
    row128 = lax.broadcasted_iota(jnp.int32, (LANES, LANES), 0)
    outs = []
    for p in range(GLA_QK // LANES):
        ps = slice(p * LANES, (p + 1) * LANES)
        s_prev = state_ref[p]
        s_b = s_prev.astype(BF16)
        ka_p = ka[:, ps].astype(BF16)
        ke_t = ke[:, ps].T.astype(BF16)
        upd = []
        for hh in range(2):
            h = 2 * p + hh
            in_head = (lane >= hh * GLA_DK) & (lane < (hh + 1) * GLA_DK)
            v_h = v[:, h * GLA_DV:(h + 1) * GLA_DV].astype(BF16)
            att = lax.dot_general(jnp.where(in_head, qa[:, ps], 0.0).astype(BF16), ka_p, NT_DIMS,
                                  preferred_element_type=F32)
            att = jnp.where(keep, att, 0.0)
            o_h = (jnp.dot(att.astype(BF16), v_h, preferred_element_type=F32)
                   + jnp.dot(jnp.where(in_head, qe[:, ps], 0.0).astype(BF16), s_b, preferred_element_type=F32))
            outs.append(o_h)
            upd.append(jnp.dot(ke_t, v_h, preferred_element_type=F32))
        d_col = jnp.broadcast_to(decay[:, ps], (LANES, LANES)).T
        state_ref[p] = s_prev * d_col + jnp.where(row128 < GLA_DK, upd[0], upd[1])
    return jnp.concatenate(outs, axis=1)


def _gla_kernel(qf_ref, kf_ref, vf_ref, af_ref, cosf_ref, sinf_ref,
                qb_ref, kb_ref, vb_ref, ab_ref, cosb_ref, sinb_ref,
                wa_ref, ba_ref, of_ref, ob_ref, sf_ref, sb_ref):
    n = pl.program_id(1)

    @pl.when(n == 0)
    def _():
        sf_ref[...] = jnp.zeros(sf_ref.shape, F32)
        sb_ref[...] = jnp.zeros(sb_ref.shape, F32)

    nch = SUB // GLA_CHUNK
    for j in range(nch):
        rf = slice(j * GLA_CHUNK, (j + 1) * GLA_CHUNK)
        of_ref[0, rf, :] = _gla_chunk(qf_ref[0, rf, :], kf_ref[0, rf, :], vf_ref[0, rf, :], af_ref[0, rf, :],
                                      cosf_ref[rf, :], sinf_ref[rf, :], wa_ref[0], ba_ref[0:1, :], sf_ref, False)
        jb = nch - 1 - j
        rb = slice(jb * GLA_CHUNK, (jb + 1) * GLA_CHUNK)
        ob_ref[0, rb, :] = _gla_chunk(qb_ref[0, rb, :], kb_ref[0, rb, :], vb_ref[0, rb, :], ab_ref[0, rb, :],
                                      cosb_ref[rb, :], sinb_ref[rb, :], wa_ref[1], ba_ref[1:2, :], sb_ref, True)


def gla_branch(proj, wa2, ba, nb):
    B, L, _ = proj.shape
    cos, sin = gla_rope_tables(SUB, L - SUB)
    wa = jnp.zeros((2, LANES, GLA_QK), F32)
    wa = wa.at[0, :GLA_RANK].set(wa2[0]).at[1, GLA_RANK:2 * GLA_RANK].set(wa2[1]).astype(BF16)

    def fwd(b, n):
        return (b, n)

    def bwd(b, n):
        return (b, jnp.where(n == 0, 0, nb - n))

    def specs(f):
        def col(width, offset):
            return pl.BlockSpec((1, SUB, width), lambda b, n: f(b, n) + (offset // width,))
        tab = pl.BlockSpec((SUB, GLA_QK), lambda b, n: (f(b, n)[1], 0))
        return [col(GLA_QK, COL_Q), col(GLA_QK, COL_K), col(GLA_V, COL_V), col(LANES, COL_A), tab, tab]

    return pl.pallas_call(
        _gla_kernel,
        grid=(B, nb),
        in_specs=specs(fwd) + specs(bwd) + [pl.BlockSpec((2, LANES, GLA_QK), lambda b, n: (0, 0, 0)),
                                            pl.BlockSpec((2, GLA_QK), lambda b, n: (0, 0))],
        out_specs=[pl.BlockSpec((1, SUB, GLA_V), lambda b, n: fwd(b, n) + (0,)),
                   pl.BlockSpec((1, SUB, GLA_V), lambda b, n: bwd(b, n) + (0,))],
        out_shape=[jax.ShapeDtypeStruct((B, L, GLA_V), F32), jax.ShapeDtypeStruct((B, L, GLA_V), F32)],
        scratch_shapes=[pltpu.VMEM((GLA_QK // LANES, LANES, GLA_DV), F32),
                        pltpu.VMEM((GLA_QK // LANES, LANES, GLA_DV), F32)],
        compiler_params=_params("parallel", "arbitrary"),
        name="gla_scan",
    )(proj, proj, proj, proj, cos, sin, proj, proj, proj, proj, cos, sin, wa, ba)


def na_key_row_start(x, rows):
    return min(max(NA_QROWS * x - NA_KH // 2, 0), rows - NA_KROWS)


def na_bias_blocks(rpb, rows):
    W = GRID_W
    rs = np.clip(np.arange(rows) - NA_KH // 2, 0, rows - NA_KH)
    cs = np.clip(np.arange(W) - NA_KW // 2, 0, W - NA_KW)
    nrb = rows // NA_QROWS
    blocks = (0, 1, nrb - 1)
    qr = np.stack([np.arange(NA_QROWS) + NA_QROWS * x for x in blocks])
    kr = np.stack([na_key_row_start(x, rows) + np.arange(NA_KROWS) for x in blocks])
    ok_r = (kr[:, None, :] >= rs[qr][:, :, None]) & (kr[:, None, :] < rs[qr][:, :, None] + NA_KH)
    i_r = np.clip(kr[:, None, :] - qr[:, :, None] + NA_KH - 1, 0, 2 * NA_KH - 2)
    qc = np.arange(W)
    kc = np.arange(W)
    ok_c = (kc[None, :] >= cs[qc][:, None]) & (kc[None, :] < cs[qc][:, None] + NA_KW)
    i_c = np.clip(kc[None, :] - qc[:, None] + NA_KW - 1, 0, 2 * NA_KW - 2)
    ok = ok_r[:, :, None, :, None] & ok_c[None, None, :, None, :]
    pick_r = (i_r[..., None] == np.arange(2 * NA_KH - 1)).astype(np.float32)
    pick_c = (i_c[..., None] == np.arange(2 * NA_KW - 1)).astype(np.float32)
    rows_sel = jnp.einsum('tajr,hrc->htajc', pick_r, rpb.astype(F32), precision=lax.Precision.HIGHEST)
    vals = jnp.einsum('htajc,qkc->htaqjk', rows_sel, pick_c, precision=lax.Precision.HIGHEST)
    bias = jnp.where(ok[None], vals, -jnp.inf)
    return bias.reshape(rpb.shape[0], 3, NA_QROWS * W, NA_KROWS * W)


def _na_kernel(q_ref, k_ref, v_ref, bias_ref, o_ref, *, rows):
    n = pl.program_id(2)
    W = GRID_W
    q = q_ref[0].astype(F32) * np.float32(NA_HD ** -0.5)
    kc = k_ref[0, 0:SUB, :].astype(BF16)
    vc = v_ref[0, 0:SUB, :].astype(BF16)
    lane = lax.broadcasted_iota(jnp.int32, (SUB, LANES), 1)

    def head_q(hh):
        in_head = (lane >= hh * NA_HD) & (lane < (hh + 1) * NA_HD)
        return jnp.where(in_head, q, 0.0).astype(BF16)

    @pl.when(n == 0)
    def _():
        outs = []
        for hh in range(2):
            sc = lax.dot_general(head_q(hh), kc, NT_DIMS, preferred_element_type=F32)
            pc = jnp.exp(sc - jnp.max(sc, axis=-1, keepdims=True))
            o = jnp.dot(pc.astype(BF16), vc, preferred_element_type=F32)
            outs.append(o / jnp.sum(pc, axis=-1, keepdims=True))
        o_ref[0] = jnp.where(lane < NA_HD, outs[0], outs[1]).astype(BF16)

    @pl.when(n > 0)
    def _():
        start_row = jnp.minimum(jnp.maximum(NA_QROWS * (n - 1) - NA_KH // 2, 0), rows - NA_KROWS)
        win = pl.ds(pl.multiple_of(SUB + start_row * W, W), NA_KROWS * W)
        kw = k_ref[0, win, :].astype(BF16)
        vw = v_ref[0, win, :].astype(BF16)
        outs = []
        for hh in range(2):
            qm = head_q(hh)
            s = lax.dot_general(qm, kw, NT_DIMS, preferred_element_type=F32) + bias_ref[hh, 0]
            sc = lax.dot_general(qm, kc, NT_DIMS, preferred_element_type=F32)
            m = jnp.maximum(jnp.max(s, axis=-1, keepdims=True), jnp.max(sc, axis=-1, keepdims=True))
            p = jnp.exp(s - m)
            pc = jnp.exp(sc - m)
            l = jnp.sum(p, axis=-1, keepdims=True) + jnp.sum(pc, axis=-1, keepdims=True)
            o = (jnp.dot(p.astype(BF16), vw, preferred_element_type=F32)
                 + jnp.dot(pc.astype(BF16), vc, preferred_element_type=F32))
            outs.append(o / l)
        o_ref[0] = jnp.where(lane < NA_HD, outs[0], outs[1]).astype(BF16)


def na_branch(proj, rpb, nb):
    B, L, _ = proj.shape
    rows = (L - SUB) // GRID_W
    nrb = nb - 1
    bias = na_bias_blocks(rpb, rows)

    def bias_idx(p, b, n):
        return (p, jnp.where(n <= 1, 0, jnp.where(n == nrb, 2, 1)), 0, 0)

    return pl.pallas_call(
        functools.partial(_na_kernel, rows=rows),
        grid=(NA_DIM // LANES, B, nb),
        in_specs=[pl.BlockSpec((1, SUB, LANES), lambda p, b, n: (b, n, COL_NQ // LANES + p)),
                  pl.BlockSpec((1, L, LANES), lambda p, b, n: (b, 0, COL_NK // LANES + p)),
                  pl.BlockSpec((1, L, LANES), lambda p, b, n: (b, 0, COL_NV // LANES + p)),
                  pl.BlockSpec((2, 1, NA_QROWS * GRID_W, NA_KROWS * GRID_W), bias_idx)],
        out_specs=pl.BlockSpec((1, SUB, LANES), lambda p, b, n: (b, n, p)),
        out_shape=jax.ShapeDtypeStruct((B, L, NA_DIM), BF16),
        compiler_params=_params("parallel", "parallel", "arbitrary"),
        name="na_attention",
    )(proj, proj, proj, bias)


def _merge_kernel(conv_ref, of_ref, ob_ref, pg_ref, na_ref, br_ref, x_ref, mod_ref, wc_ref, wg_ref, wn_ref,
                  wo_ref, bg_ref, gn_ref, n2_ref, xo_ref, h2_ref, *, nb, nbatch):
    r = pl.program_id(0)
    o = of_ref[...] + ob_ref[...]
    on = jnp.concatenate([_rms(o[:, h * GLA_DV:(h + 1) * GLA_DV]) for h in range(GLA_HEADS)], axis=1) * gn_ref[...]
    pg = pg_ref[...].astype(F32)
    y_gla = jnp.dot((on * (pg * _sigmoid(pg))).astype(BF16), wg_ref[...], preferred_element_type=F32)
    y_conv = jnp.dot(conv_ref[...], wc_ref[...], preferred_element_type=F32)
    y_na = jnp.dot(na_ref[...], wn_ref[...], preferred_element_type=F32)
    def gate(k):
        cols = slice(k * D_MODEL, (k + 1) * D_MODEL)
        return _sigmoid(br_ref[:, cols].astype(F32) + bg_ref[:, cols])

    m = gate(0) * y_conv + gate(1) * y_gla + gate(2) * y_na
    y = jnp.dot(m.astype(BF16), wo_ref[...], preferred_element_type=F32)
    for s in range(ROW_TILE // SUB):
        rows = slice(s * SUB, (s + 1) * SUB)
        md = mod_ref[_mod_index(r * (ROW_TILE // SUB) + s, nb, nbatch)]
        x = x_ref[rows, :] + md[2:3, :] * y[rows, :]
        xo_ref[rows, :] = x
        h2_ref[rows, :] = (_rms(x) * n2_ref[...] * (1.0 + md[4:5, :]) + md[3:4, :]).astype(BF16)


def merge_branches(conv_act, o_f, o_b, y_na, proj, x, mod, w_conv_o, w_gla_o, w_na_o, w_out, b_gate,
                   gla_norm_g, norm2_g, nb, nbatch):
    R = x.shape[0]

    def rows(width, block=0):
        return pl.BlockSpec((ROW_TILE, width), lambda r: (r, block))

    def whole(shape):
        return pl.BlockSpec(shape, lambda r: (0,) * len(shape))

    return pl.pallas_call(
        functools.partial(_merge_kernel, nb=nb, nbatch=nbatch),
        grid=(R // ROW_TILE,),
        in_specs=[rows(CONV_DIM), rows(GLA_V), rows(GLA_V), rows(GLA_V, COL_G // GLA_V), rows(NA_DIM),
                  rows(N_BRANCH * D_MODEL, COL_BR // (N_BRANCH * D_MODEL)), rows(D_MODEL),
                  whole((nbatch + 1, SUBLANES, D_MODEL)),
                  whole((CONV_DIM, D_MODEL)), whole((GLA_V, D_MODEL)), whole((NA_DIM, D_MODEL)),
                  whole((D_MODEL, D_MODEL)), whole((1, N_BRANCH * D_MODEL)), whole((1, GLA_V)),
                  whole((1, D_MODEL))],
        out_specs=[rows(D_MODEL), rows(D_MODEL)],
        out_shape=[jax.ShapeDtypeStruct((R, D_MODEL), F32), jax.ShapeDtypeStruct((R, D_MODEL), BF16)],
        compiler_params=_params("parallel"),
        name="merge_branches",
    )(conv_act, o_f, o_b, proj, y_na, proj, x, mod, w_conv_o.astype(BF16), w_gla_o.astype(BF16),
      w_na_o.astype(BF16), w_out.astype(BF16), b_gate.reshape(1, -1), gla_norm_g.reshape(1, -1),
      norm2_g.reshape(1, -1))


def _peer_route_kernel(h_ref, wq_ref, keys_ref, st_ref, stats_ref, *, nh, topk):
    tt = h_ref.shape[0]
    qry = jnp.dot(h_ref[...], wq_ref[...], preferred_element_type=F32)
    neg = jnp.float32(-jnp.inf)
    for h in range(nh):
        tops = []
        for z in range(2):
            hz = 2 * h + z
            q = qry[:, hz * LANES:(hz + 1) * LANES].astype(BF16)
            s = lax.dot_general(keys_ref[hz], q, NT_DIMS, preferred_element_type=F32)
            st_ref[hz] = s
            cur = s
            vals = []
            for _ in range(topk):
                m = jnp.max(cur, axis=0, keepdims=True)
                vals.append(m)
                cur = jnp.where(cur == m, neg, cur)
            tops.append(vals)
        a, b = tops
        cands = [a[p] + b[q] for p in range(topk) for q in range(topk) if (p + 1) * (q + 1) <= topk]
        npad = (-len(cands)) % SUBLANES
        cand = jnp.concatenate(cands + [jnp.full((npad, tt), neg, F32)], axis=0)
        cur = cand
        tau = None
        for _ in range(topk):
            tau = jnp.max(cur, axis=0, keepdims=True)
            cur = jnp.where(cur == tau, neg, cur)
        top = a[0] + b[0]
        zsum = jnp.sum(jnp.where(cand >= tau, jnp.exp(cand - top), 0.0), axis=0, keepdims=True)
        stats_ref[h] = jnp.concatenate([tau, a[0], 1.0 / zsum, jnp.zeros((SUBLANES - 3, tt), F32)] + b, axis=0)


def _peer_main_kernel(hb_ref, st_ref, stats_ref, u_ref, vt_ref, out_ref, acc_ref, e0_ref, cnt_ref, e1_ref, rank_ref,
                      *, nh, eb, tt, topk):
    i = pl.program_id(1)

    @pl.when(i == 0)
    def _():
        acc_ref[...] = jnp.zeros(acc_ref.shape, F32)
        for h in range(nh):
            tau = stats_ref[h, 0:1, :]
            a1 = stats_ref[h, 1:2, :]
            rz = stats_ref[h, 2:3, :]
            s0 = st_ref[2 * h]
            s1 = st_ref[2 * h + 1]
            e0_ref[h] = (jnp.exp(s0 - a1) * rz).astype(BF16)
            e1_ref[h] = jnp.exp(s1 - stats_ref[h, SUBLANES:SUBLANES + 1, :]).astype(BF16)
            cnt = jnp.zeros(s0.shape, F32)
            rank = jnp.zeros(s1.shape, F32)
            for q in range(topk):
                bq = stats_ref[h, SUBLANES + q:SUBLANES + q + 1, :]
                cnt = cnt + jnp.where(s0 + bq >= tau, 1.0, 0.0)
                rank = rank + jnp.where(bq > s1, 1.0, 0.0)
            cnt_ref[h] = cnt.astype(BF16)
            rank_ref[h] = rank.astype(BF16)

    nsub = eb // LANES
    assert nsub % (2 * SUBLANES) == 0
    first = pl.ds(pl.multiple_of(i * nsub, 2 * SUBLANES), nsub)
    nchunk = tt // PEER_CHUNK

    def act_of(c):
        return lax.dot_general(u_ref[...], hb_ref[c * PEER_CHUNK:(c + 1) * PEER_CHUNK, :], NT_DIMS,
                               preferred_element_type=F32)

    act_next = act_of(0)
    for c in range(nchunk):
        cs = slice(c * PEER_CHUNK, (c + 1) * PEER_CHUNK)
        act = act_next
        if c + 1 < nchunk:
            act_next = act_of(c + 1)
        e0 = [e0_ref[h, first, cs].astype(F32) for h in range(nh)]
        cnt = [cnt_ref[h, first, cs].astype(F32) for h in range(nh)]
        ws = []
        for s in range(nsub):
            g = jnp.zeros((LANES, PEER_CHUNK), BF16)
            for h in range(nh):
                keep = rank_ref[h, :, cs] < cnt[h][s:s + 1].astype(BF16)
                g = g + jnp.where(keep, e1_ref[h, :, cs] * e0[h][s:s + 1].astype(BF16), jnp.zeros((), BF16))
            a = act[s * LANES:(s + 1) * LANES]
            ge = 0.5 * a * (1.0 + lax.erf(a * np.float32(math.sqrt(0.5))))
            ws.append(g * ge.astype(BF16))
        w = jnp.concatenate(ws, axis=0)
        acc_ref[:, cs] += jnp.dot(vt_ref[...], w, preferred_element_type=F32)

    @pl.when(i == pl.num_programs(1) - 1)
    def _():
        out_ref[...] = acc_ref[...].T


def peer_ffn(h2, wq, keys, u_tab, v_tab):
    T, D = h2.shape
    tt, eb, topk = PEER_TT, PEER_EB, PEER_TOPK
    nh = keys.shape[0]
    nk = keys.shape[2]
    assert nk == LANES and keys.shape[3] == LANES and T % tt == 0
    E = nk * nk
    wq_b = wq.astype(BF16)
    keys_b = keys.reshape(2 * nh, nk, LANES).astype(BF16)
    nt = T // tt
    st, stats = pl.pallas_call(
        functools.partial(_peer_route_kernel, nh=nh, topk=topk),
        grid=(nt,),
        in_specs=[pl.BlockSpec((tt, D), lambda t: (t, 0)),
                  pl.BlockSpec((D, 2 * nh * LANES), lambda t: (0, 0)),
                  pl.BlockSpec((2 * nh, nk, LANES), lambda t: (0, 0, 0))],
        out_specs=[pl.BlockSpec((2 * nh, nk, tt), lambda t: (0, 0, t)),
                   pl.BlockSpec((nh, SUBLANES + topk, tt), lambda t: (0, 0, t))],
        out_shape=[jax.ShapeDtypeStruct((2 * nh, nk, T), F32),
                   jax.ShapeDtypeStruct((nh, SUBLANES + topk, T), F32)],
        compiler_params=_params("parallel"),
        name="peer_route",
    )(h2, wq_b, keys_b)
    u_b = u_tab.astype(BF16)
    vt_b = v_tab.T.astype(BF16)
    return pl.pallas_call(
        functools.partial(_peer_main_kernel, nh=nh, eb=eb, tt=tt, topk=topk),
        grid=(nt, E // eb),
        in_specs=[pl.BlockSpec((tt, D), lambda t, i: (t, 0), pipeline_mode=pl.Buffered(1)),
                  pl.BlockSpec((2 * nh, nk, tt), lambda t, i: (0, 0, t), pipeline_mode=pl.Buffered(1)),
                  pl.BlockSpec((nh, SUBLANES + topk, tt), lambda t, i: (0, 0, t)),
                  pl.BlockSpec((eb, D), lambda t, i: (i, 0)),
                  pl.BlockSpec((D, eb), lambda t, i: (0, i))],
        out_specs=pl.BlockSpec((tt, D), lambda t, i: (t, 0)),
        out_shape=jax.ShapeDtypeStruct((T, D), F32),
        scratch_shapes=[pltpu.VMEM((D, tt), F32)] + [pltpu.VMEM((nh, nk, tt), BF16)] * 4,
        compiler_params=_params("parallel", "arbitrary"),
        name="peer_main",
    )(h2, st, stats, u_b, vt_b)


def _final_kernel(x_ref, p_ref, mod_ref, g_ref, o_ref):
    b = pl.program_id(0)
    x = x_ref[0] + mod_ref[b][5:6, :] * p_ref[0]
    o_ref[0] = _rms(x) * g_ref[...]


def final_norm(x, p, mod, g, nbatch):
    B, L, D = x.shape
    blk = pl.BlockSpec((1, SUB, D), lambda b, n: (b, n + 1, 0))
    return pl.pallas_call(
        _final_kernel,
        grid=(B, L // SUB - 1),
        in_specs=[blk, blk, pl.BlockSpec((nbatch + 1, SUBLANES, D), lambda b, n: (0, 0, 0)),
                  pl.BlockSpec((1, D), lambda b, n: (0, 0))],
        out_specs=pl.BlockSpec((1, SUB, D), lambda b, n: (b, n, 0)),
        out_shape=jax.ShapeDtypeStruct((B, L - SUB, D), F32),
        compiler_params=_params("parallel", "parallel"),
        name="final_norm",
    )(x, p, mod, g.reshape(1, D))


def kernel(x, c, ctx, c_ctx, w_mod, b_mod, norm1_g, norm2_g, w_in, b_gate, conv_w, conv_b,
           conv_ln_g, conv_ln_b, w_conv_o, gla_wa2, gla_ba, gla_norm_g, w_gla_o, na_rpb, w_na_o,
           w_out, peer_wq, peer_keys, peer_u, peer_v, final_g):
    B, S, D = x.shape
    L = ctx.shape[1] + S
    assert ctx.shape[1] == SUB and S % SUB == 0 and (B * L) % PEER_TT == 0 and (B * L) % INPROJ_TILE == 0
    nb = L // SUB
    R = B * L
    xs = jnp.concatenate([ctx, x], axis=1).reshape(R, D)
    p = None
    mod_prev = None
    for i in range(DEPTH):
        mod = mod_table(c, c_ctx, w_mod[i], b_mod[i])
        proj, xs = in_projection(xs, p, mod_prev, mod, norm1_g[i], reorder_w_in(w_in[i]), nb, B)
        proj3 = proj.reshape(B, L, IN_COLS)
        conv_act = conv_branch(proj3, conv_w[i], conv_b[i], conv_ln_g[i], conv_ln_b[i], nb)
        o_f, o_b = gla_branch(proj3, gla_wa2[i], gla_ba[i], nb)
        y_na = na_branch(proj3, na_rpb[i], nb)
        xs, h2 = merge_branches(conv_act.reshape(R, CONV_DIM), o_f.reshape(R, GLA_V), o_b.reshape(R, GLA_V),
                                y_na.reshape(R, NA_DIM), proj, xs, mod, w_conv_o[i], w_gla_o[i], w_na_o[i],
                                w_out[i], b_gate[i], gla_norm_g[i], norm2_g[i], nb, B)
        p = peer_ffn(h2, peer_wq[i], peer_keys[i], peer_u[i], peer_v[i])
        mod_prev = mod
    return final_norm(xs.reshape(B, L, D), p.reshape(B, L, D), mod_prev, final_g, B)
```
